```python
import math
import jax, jax.numpy as jnp
from jax import lax
import numpy as np

D_MODEL = 2048
BATCH = 8
SEQ = 4096
DEPTH = 4

CTX_LEN = 256
GRID_W = 64
N_MOD = 9
FFN_DIM = 5632
HEAD_DIM = 128
NA_HEADS = (D_MODEL // 2) // HEAD_DIM
NA_WIN_ROWS = 8
NA_WIN_COLS = 16
NA_QCOLS = 16
DIFF_HEADS = (D_MODEL // 2) // (2 * HEAD_DIM)
DIFF_QBLOCK = 128
ROPE_THETA = 10000.0
SSM_INNER = 2 * D_MODEL
SSM_HEADDIM = 64
SSM_HEADS = SSM_INNER // SSM_HEADDIM
SSM_STATE = 128
SSM_GROUPS = 8
SSM_CONV = 4
SSM_CHUNK = 128
SSM_CONV_CH = SSM_INNER + 2 * SSM_GROUPS * SSM_STATE
SSM_IN_W = SSM_INNER + SSM_CONV_CH + 2 * SSM_HEADS
ATTN_IN_W = 3 * NA_HEADS * HEAD_DIM + 3 * DIFF_HEADS * 2 * HEAD_DIM
N_EVEN = (DEPTH + 1) // 2
N_ODD = DEPTH // 2
NEG_BIG = -1e30

kernel_name = "hybrid_natten_diffattn_ssd_dit_trunk"


def rmsnorm(x, g, eps=1e-6):
    xf = x.astype(jnp.float32)
    y = xf * lax.rsqrt(jnp.mean(xf * xf, axis=-1, keepdims=True) + eps)
    return y.astype(x.dtype) * g


def modulate(x, g, shift, scale):
    return rmsnorm(x, g) * (1 + scale) + shift


def swiglu(h, w1, w3, w2):
    return (jax.nn.silu(h @ w1) * (h @ w3)) @ w2


def axial_rope_tables(seq, dtype):
    quarter = HEAD_DIM // 4
    inv = 1.0 / (ROPE_THETA ** (jnp.arange(quarter, dtype=jnp.float32) / quarter))
    t = jnp.arange(seq)
    row = (t // GRID_W).astype(jnp.float32)[:, None] * inv
    col = (t % GRID_W).astype(jnp.float32)[:, None] * inv
    ang = jnp.concatenate([row, row, col, col], axis=-1)
    return jnp.cos(ang).astype(dtype), jnp.sin(ang).astype(dtype)


def apply_axial_rope(x, cos, sin):
    xr = x.reshape(x.shape[:-1] + (2, 2, HEAD_DIM // 4))
    rot = jnp.stack([-xr[..., 1, :], xr[..., 0, :]], axis=-2).reshape(x.shape)
    return x * cos[:, None, None, :] + rot * sin[:, None, None, :]


def na_static(kr):
    ncb = GRID_W // NA_QCOLS
    span = NA_QCOLS + NA_WIN_COLS
    cb = np.arange(ncb) * NA_QCOLS
    g0 = np.clip(cb - NA_WIN_COLS // 2, 0, GRID_W - span)
    key_cols = g0[:, None] + np.arange(span)[None, :]
    q_cols = cb[:, None] + np.arange(NA_QCOLS)[None, :]
    cs = np.clip(q_cols - NA_WIN_COLS // 2, 0, GRID_W - NA_WIN_COLS)
    kc = key_cols[:, None, :]
    ok = (kc >= cs[:, :, None]) & (kc < cs[:, :, None] + NA_WIN_COLS)
    coff = np.clip(kc - q_cols[:, :, None], -(NA_WIN_COLS - 1), NA_WIN_COLS - 1) + NA_WIN_COLS - 1
    mask = np.broadcast_to(ok[:, :, None, :], (ncb, NA_QCOLS, kr, span)).reshape(ncb, NA_QCOLS, kr * span)
    return key_cols, coff, mask


def neighbourhood_attn(q, k, v, kc, vc, rpb):
    b, s, h, d = q.shape
    rows = s // GRID_W
    kr = min(NA_WIN_ROWS, rows)
    ncb = GRID_W // NA_QCOLS
    span = NA_QCOLS + NA_WIN_COLS
    key_cols, coff, mask = na_static(kr)
    qg = q.reshape(b, rows, GRID_W, h, d)
    kg = k.reshape(b, rows, GRID_W, h, d)
    vg = v.reshape(b, rows, GRID_W, h, d)
    scale = HEAD_DIM ** -0.5
    nw = kr * span

    def row_fn(r):
        qr = lax.dynamic_index_in_dim(qg, r, axis=1, keepdims=False).reshape(b, ncb, NA_QCOLS, h, d)
        rs = jnp.clip(r - kr // 2, 0, rows - kr)
        kw = lax.dynamic_slice_in_dim(kg, rs, kr, axis=1)[:, :, key_cols]
        vw = lax.dynamic_slice_in_dim(vg, rs, kr, axis=1)[:, :, key_cols]
        kw = kw.transpose(0, 2, 1, 3, 4, 5).reshape(b, ncb, nw, h, d)
        vw = vw.transpose(0, 2, 1, 3, 4, 5).reshape(b, ncb, nw, h, d)
        roff = rs + jnp.arange(kr) - r + (NA_WIN_ROWS - 1)
        bias = rpb[:, roff][:, :, coff]
        bias = bias.transpose(0, 2, 3, 1, 4).reshape(h, ncb, NA_QCOLS, nw).astype(jnp.float32)
        s_win = jnp.einsum('bnqhd,bnkhd->bhnqk', qr, kw).astype(jnp.float32) * scale + bias
        s_win = jnp.where(mask, s_win, NEG_BIG)
        s_ctx = jnp.einsum('bnqhd,bkhd->bhnqk', qr, kc).astype(jnp.float32) * scale
        p = jax.nn.softmax(jnp.concatenate([s_win, s_ctx], axis=-1), axis=-1).astype(v.dtype)
        o = (jnp.einsum('bhnqk,bnkhd->bnqhd', p[..., :nw], vw)
             + jnp.einsum('bhnqk,bkhd->bnqhd', p[..., nw:], vc))
        return o.reshape(b, GRID_W, h, d)

    out = lax.map(row_fn, jnp.arange(rows))
    return out.transpose(1, 0, 2, 3, 4).reshape(b, s, h * d)


def context_attn(q, k, v):
    s = jnp.einsum('bqhd,bkhd->bhqk', q, k).astype(jnp.float32) * HEAD_DIM ** -0.5
    p = jax.nn.softmax(s, axis=-1).astype(v.dtype)
    return jnp.einsum('bhqk,bkhd->bqhd', p, v)


def diff_attn(q, k, v, lam):
    s = jnp.einsum('bqhmd,bkhmd->bhmqk', q, k).astype(jnp.float32) * HEAD_DIM ** -0.5
    p = jax.nn.softmax(s, axis=-1)
    pd = (p[:, :, 0] - lam * p[:, :, 1]).astype(v.dtype)
    return jnp.einsum('bhqk,bkhe->bqhe', pd, v)


def hybrid_attention(hx, hc, w_in, w_out, rpb, lam_vecs, subln_g, lam_init, cos, sin, ctx_out):
    b, s, _ = hx.shape
    l = hc.shape[1]
    na_w = NA_HEADS * HEAD_DIM
    df_w = DIFF_HEADS * 2 * HEAD_DIM
    cuts = [na_w, 2 * na_w, 3 * na_w, 3 * na_w + df_w, 3 * na_w + 2 * df_w]

    def split(p):
        bb, n = p.shape[0], p.shape[1]
        aq, ak, av, dq, dk, dv = jnp.split(p, cuts, axis=-1)
        sa = (bb, n, NA_HEADS, HEAD_DIM)
        sd = (bb, n, DIFF_HEADS, 2, HEAD_DIM)
        return (aq.reshape(sa), ak.reshape(sa), av.reshape(sa),
                dq.reshape(sd), dk.reshape(sd), dv.reshape(bb, n, DIFF_HEADS, 2 * HEAD_DIM))

    aqx, akx, avx, dqx, dkx, dvx = split(hx @ w_in)
    aqc, akc, avc, dqc, dkc, dvc = split(hc @ w_in)

    na_x = neighbourhood_attn(aqx, akx, avx, akc, avc, rpb)

    lf = lam_vecs.astype(jnp.float32)
    lam = jnp.exp(jnp.sum(lf[0] * lf[1])) - jnp.exp(jnp.sum(lf[2] * lf[3])) + lam_init
    dqx = apply_axial_rope(dqx, cos, sin)
    dkx = apply_axial_rope(dkx, cos, sin)
    k_all = jnp.concatenate([dkx, dkc], axis=1)
    v_all = jnp.concatenate([dvx, dvc], axis=1)
    qb = dqx.reshape(b, s // DIFF_QBLOCK, DIFF_QBLOCK, DIFF_HEADS, 2, HEAD_DIM).swapaxes(0, 1)
    dfx = lax.map(lambda qq: diff_attn(qq, k_all, v_all, lam), qb)
    dfx = dfx.swapaxes(0, 1).reshape(b, s, DIFF_HEADS, 2 * HEAD_DIM)

    def diff_out(o):
        return (rmsnorm(o, subln_g) * (1 - lam_init)).reshape(o.shape[0], o.shape[1], df_w)

    out_x = jnp.concatenate([na_x, diff_out(dfx)], axis=-1) @ w_out
    if not ctx_out:
        return out_x, None
    na_c = context_attn(aqc, akc, avc).reshape(b, l, na_w)
    dfc = diff_attn(dqc, dkc, dvc, lam)
    out_c = jnp.concatenate([na_c, diff_out(dfc)], axis=-1) @ w_out
    return out_x, out_c


def depthwise_conv(x, w, bias):
    k = w.shape[0]
    left = (k - 1) // 2
    y = lax.conv_general_dilated(x, w[:, None, :], window_strides=(1,), padding=[(left, k - 1 - left)],
                                 dimension_numbers=('NWC', 'WIO', 'NWC'), feature_group_count=x.shape[-1])
    return y + bias


def ssd_chunked(X, a, Bm, Cm, h0, want_y):
    b, l, h, p = X.shape
    g, n = Bm.shape[-2:]
    j = h // g
    T = SSM_CHUNK
    c = l // T
    X = X.reshape(b, c, T, g, j, p)
    a_cs = jnp.cumsum(a.reshape(b, c, T, g, j), axis=2)
    Bc = Bm.astype(jnp.float32).reshape(b, c, T, g, n)
    Cc = Cm.astype(jnp.float32).reshape(b, c, T, g, n)
    decay_to_end = jnp.exp(a_cs[:, :, -1:] - a_cs)
    states = jnp.einsum('bcsgn,bcsgj,bcsgjp->bcgjpn', Bc, decay_to_end, X)
    chunk_decay = jnp.exp(a_cs[:, :, -1])

    def step(hc, inp):
        dec, st = inp
        return dec[..., None, None] * hc + st, hc

    h_fin, h_prev = lax.scan(step, h0.reshape(b, g, j, p, n),
                             (chunk_decay.transpose(1, 0, 2, 3), states.transpose(1, 0, 2, 3, 4, 5)))
    h_fin = h_fin.reshape(b, h, p, n)
    if not want_y:
        return None, h_fin
    h_prev = h_prev.transpose(1, 0, 2, 3, 4, 5)
    seg = a_cs[:, :, :, None] - a_cs[:, :, None, :]
    lower = np.tril(np.ones((T, T), dtype=bool))[None, None, :, :, None, None]
    Lmat = jnp.exp(jnp.where(lower, seg, -jnp.inf))
    cb = jnp.einsum('bclgn,bcsgn->bclsg', Cc, Bc)
    y_diag = jnp.einsum('bclsg,bclsgj,bcsgjp->bclgjp', cb, Lmat, X)
    y_off = jnp.einsum('bclgn,bcgjpn,bclgj->bclgjp', Cc, h_prev, jnp.exp(a_cs))
    return (y_diag + y_off).reshape(b, l, h, p), h_fin


def ssd_direction(xs, bm, cm, dt_raw, h0, a_log, dt_bias, d_skip, want_y):
    dt = jax.nn.softplus(dt_raw.astype(jnp.float32) + dt_bias.astype(jnp.float32))
    a = dt * (-jnp.exp(a_log.astype(jnp.float32)))
    xf = xs.astype(jnp.float32)
    y, h_fin = ssd_chunked(xf * dt[..., None], a, bm, cm, h0, want_y)
    if want_y:
        y = (y + d_skip.astype(jnp.float32)[:, None] * xf).astype(xs.dtype)
    return y, h_fin


def maybe_flip(t, rev):
    return jnp.flip(t, axis=1) if rev else t


def bidir_ssd(hx, hc, w_in, conv_w, conv_b, a_log, dt_bias, d_skip, norm_g, w_out, ctx_out):
    def prep(hh):
        bb, n = hh.shape[:2]
        z, xbc, dt = jnp.split(hh @ w_in, [SSM_INNER, SSM_INNER + SSM_CONV_CH], axis=-1)
        xbc = jax.nn.silu(depthwise_conv(xbc, conv_w, conv_b))
        xs, bm, cm = jnp.split(xbc, [SSM_INNER, SSM_INNER + SSM_GROUPS * SSM_STATE], axis=-1)
        return (z, xs.reshape(bb, n, SSM_HEADS, SSM_HEADDIM),
                bm.reshape(bb, n, SSM_GROUPS, SSM_STATE), cm.reshape(bb, n, SSM_GROUPS, SSM_STATE), dt)

    zx, xx, bx, cx, dtx = prep(hx)
    zc, xc, bc, cc, dtc = prep(hc)
    h0 = jnp.zeros((hx.shape[0], SSM_HEADS, SSM_HEADDIM, SSM_STATE), jnp.float32)
    ys_lat, ys_ctx = [], []
    for d in range(2):
        rev = d == 1
        sl = slice(d * SSM_HEADS, (d + 1) * SSM_HEADS)
        yc, hc_fin = ssd_direction(maybe_flip(xc, rev), maybe_flip(bc, rev), maybe_flip(cc, rev),
                                   maybe_flip(dtc[..., sl], rev), h0, a_log[d], dt_bias[d], d_skip[d], ctx_out)
        yx, _ = ssd_direction(maybe_flip(xx, rev), maybe_flip(bx, rev), maybe_flip(cx, rev),
                              maybe_flip(dtx[..., sl], rev), hc_fin, a_log[d], dt_bias[d], d_skip[d], True)
        ys_lat.append(maybe_flip(yx, rev))
        if ctx_out:
            ys_ctx.append(maybe_flip(yc, rev))

    def finish(ys, z):
        y = ys[0] + ys[1]
        bb, n = y.shape[:2]
        y = y.reshape(bb, n, SSM_INNER) * jax.nn.silu(z)
        y = rmsnorm(y.reshape(bb, n, SSM_GROUPS, SSM_INNER // SSM_GROUPS),
                    norm_g.reshape(SSM_GROUPS, SSM_INNER // SSM_GROUPS)).reshape(bb, n, SSM_INNER)
        return y @ w_out

    out_x = finish(ys_lat, zx)
    if not ctx_out:
        return out_x, None
    return out_x, finish(ys_ctx, zc)


def setup_inputs(seed: int = 0) -> dict:
    key = jax.random.key(seed)
    ks = jax.random.split(key, 24)
    D = D_MODEL
    f32 = jnp.float32

    def nrm(k, shape, s):
        return jax.random.normal(k, shape, f32) * s

    dt = jnp.exp(jax.random.uniform(ks[19], (N_ODD, 2, SSM_HEADS), f32, math.log(1e-3), math.log(1e-1)))
    return {
        "x": nrm(ks[0], (BATCH, SEQ, D), 1.0),
        "c": nrm(ks[1], (BATCH, D), 1.0),
        "ctx": nrm(ks[2], (BATCH, CTX_LEN, D), 1.0),
        "c_ctx": nrm(ks[3], (D,), 1.0),
        "mod_w": nrm(ks[4], (DEPTH, D, N_MOD * D), 0.5 * D ** -0.5),
        "mod_b": nrm(ks[5], (DEPTH, N_MOD * D), 0.02),
        "norm_g": 1.0 + nrm(ks[6], (DEPTH, 3, D), 0.02),
        "ffn_w1": nrm(ks[7], (DEPTH, 2, D, FFN_DIM), D ** -0.5),
        "ffn_w3": nrm(ks[8], (DEPTH, 2, D, FFN_DIM), D ** -0.5),
        "ffn_w2": nrm(ks[9], (DEPTH, 2, FFN_DIM, D), FFN_DIM ** -0.5),
        "attn_w_in": nrm(ks[10], (N_EVEN, D, ATTN_IN_W), D ** -0.5),
        "attn_w_out": nrm(ks[11], (N_EVEN, D, D), D ** -0.5),
        "na_rpb": nrm(ks[12], (N_EVEN, NA_HEADS, 2 * NA_WIN_ROWS - 1, 2 * NA_WIN_COLS - 1), 0.02),
        "diff_lambda": nrm(ks[13], (N_EVEN, 4, HEAD_DIM), 0.1),
        "diff_subln_g": 1.0 + nrm(ks[14], (N_EVEN, 2 * HEAD_DIM), 0.02),
        "ssm_w_in": nrm(ks[15], (N_ODD, D, SSM_IN_W), D ** -0.5),
        "ssm_conv_w": nrm(ks[16], (N_ODD, SSM_CONV, SSM_CONV_CH), SSM_CONV ** -0.5),
        "ssm_conv_b": nrm(ks[17], (N_ODD, SSM_CONV_CH), 0.02),
        "ssm_a_log": jnp.log(jax.random.uniform(ks[18], (N_ODD, 2, SSM_HEADS), f32, 1.0, 16.0)),
        "ssm_dt_bias": dt + jnp.log(-jnp.expm1(-dt)),
        "ssm_d": 1.0 + nrm(ks[20], (N_ODD, 2, SSM_HEADS), 0.02),
        "ssm_norm_g": 1.0 + nrm(ks[21], (N_ODD, SSM_INNER), 0.02),
        "ssm_w_out": nrm(ks[22], (N_ODD, SSM_INNER, D), SSM_INNER ** -0.5),
        "final_norm_g": 1.0 + nrm(ks[23], (D,), 0.02),
    }


def reference(x, c, ctx, c_ctx, mod_w, mod_b, norm_g, ffn_w1, ffn_w3, ffn_w2, attn_w_in, attn_w_out,
              na_rpb, diff_lambda, diff_subln_g, ssm_w_in, ssm_conv_w, ssm_conv_b, ssm_a_log, ssm_dt_bias,
              ssm_d, ssm_norm_g, ssm_w_out, final_norm_g):
    b, s, d_model = x.shape
    cos, sin = axial_rope_tables(s, x.dtype)
    h = ctx
    for layer in range(DEPTH):
        ctx_out = layer < DEPTH - 1
        m = (jax.nn.silu(c) @ mod_w[layer] + mod_b[layer]).reshape(b, 1, N_MOD, d_model)
        mc = (jax.nn.silu(c_ctx) @ mod_w[layer] + mod_b[layer]).reshape(1, 1, N_MOD, d_model)
        w1a, w3a, w2a = ffn_w1[layer, 0], ffn_w3[layer, 0], ffn_w2[layer, 0]
        w1b, w3b, w2b = ffn_w1[layer, 1], ffn_w3[layer, 1], ffn_w2[layer, 1]

        x = x + 0.5 * m[:, :, 2] * swiglu(modulate(x, norm_g[layer, 0], m[:, :, 0], m[:, :, 1]), w1a, w3a, w2a)
        h = h + 0.5 * mc[:, :, 2] * swiglu(modulate(h, norm_g[layer, 0], mc[:, :, 0], mc[:, :, 1]), w1a, w3a, w2a)

        hx = modulate(x, norm_g[layer, 1], m[:, :, 3], m[:, :, 4])
        hc = modulate(h, norm_g[layer, 1], mc[:, :, 3], mc[:, :, 4])
        i = layer // 2
        if layer % 2 == 0:
            lam_init = 0.8 - 0.6 * math.exp(-0.3 * layer)
            ox, oc = hybrid_attention(hx, hc, attn_w_in[i], attn_w_out[i], na_rpb[i], diff_lambda[i],
                                      diff_subln_g[i], lam_init, cos, sin, ctx_out)
        else:
            ox, oc = bidir_ssd(hx, hc, ssm_w_in[i], ssm_conv_w[i], ssm_conv_b[i], ssm_a_log[i],
                               ssm_dt_bias[i], ssm_d[i], ssm_norm_g[i], ssm_w_out[i], ctx_out)
        x = x + m[:, :, 5] * ox

        x = x + 0.5 * m[:, :, 8] * swiglu(modulate(x, norm_g[layer, 2], m[:, :, 6], m[:, :, 7]), w1b, w3b, w2b)
        if ctx_out:
            h = h + mc[:, :, 5] * oc
            h = h + 0.5 * mc[:, :, 8] * swiglu(modulate(h, norm_g[layer, 2], mc[:, :, 6], mc[:, :, 7]), w1b, w3b, w2b)
    return rmsnorm(x, final_norm_g)
```

```python
import functools
import math

import jax
import jax.numpy as jnp
import numpy as np
from jax import lax
from jax.experimental import pallas as pl
from jax.experimental.pallas import tpu as pltpu

F32 = jnp.float32
BF16 = jnp.bfloat16

HEAD_DIM = 128
GRID_W = 64
N_MOD = 9
NA_WIN_ROWS = 8
NA_WIN_COLS = 16
ROPE_THETA = 10000.0
SSM_HEADDIM = 64
SSM_STATE = 128
SSM_GROUPS = 8
SSM_CONV = 4
SSM_CHUNK = 128
NEG_BIG = -1e30
EPS = 1e-6

V7X_VMEM_BYTES = 64 * 1024 * 1024
V7X_LANES = 128
BF16_SUBLANES = 16
VMEM_LIMIT = (V7X_VMEM_BYTES * 7) // 8
LOG2E = math.log2(math.e)
NA_QROWS = 4
CONV_ROW_BLOCKS = 3
MOD_SLOTS = 16
CONV_HALO = BF16_SUBLANES


def _params(sem, vmem=VMEM_LIMIT):
    return pltpu.CompilerParams(dimension_semantics=sem, vmem_limit_bytes=vmem)


def _pick(n, want):
    t = min(n, want)
    while n % t:
        t -= 1
    return t


def _silu(a):
    return a * jax.nn.sigmoid(a)


def _modulate(x, g, shift, scale):
    ms = jnp.mean(x * x, axis=-1, keepdims=True)
    return (x * lax.rsqrt(ms + EPS)) * (g * (1.0 + scale)) + shift


def _mod_spec(kind, slot0, rows_per_slot, tm, d, col=False):
    if col:
        return pl.BlockSpec((None, None, 1, d), lambda i, j: (kind, slot0 + (i * tm) // rows_per_slot, 0, j))
    return pl.BlockSpec((None, None, 1, d), lambda i, j: (kind, slot0 + (i * tm) // rows_per_slot, 0, 0))


def _mod_kernel(c_ref, w_ref, b_ref, o_ref):
    a = _silu(c_ref[...])
    o_ref[...] = jnp.dot(a, w_ref[...], preferred_element_type=F32, precision=lax.Precision.HIGHEST) + b_ref[...]


def _modulation(cvec, mod_w, mod_b):
    depth, d, n = mod_w.shape
    tn = _pick(n, 1024)
    return pl.pallas_call(
        _mod_kernel,
        grid=(depth, n // tn),
        in_specs=[
            pl.BlockSpec((MOD_SLOTS, d), lambda l, j: (0, 0)),
            pl.BlockSpec((None, d, tn), lambda l, j: (l, 0, j)),
            pl.BlockSpec((None, 1, tn), lambda l, j: (l, 0, j)),
        ],
        out_specs=pl.BlockSpec((None, MOD_SLOTS, tn), lambda l, j: (l, 0, j)),
        out_shape=jax.ShapeDtypeStruct((depth, MOD_SLOTS, n), F32),
        compiler_params=_params(("parallel", "parallel")),
        name="modulation",
    )(cvec, mod_w, mod_b.reshape(depth, 1, n))


def _ffn_kernel(*refs, final):
    if final:
        x_ref, sh_ref, sc_ref, gt_ref, g_ref, w1_ref, w3_ref, w2_ref, fg_ref, o_ref, h_ref, acc_ref = refs
    else:
        x_ref, sh_ref, sc_ref, gt_ref, g_ref, w1_ref, w3_ref, w2_ref, o_ref, h_ref, acc_ref = refs
    j = pl.program_id(1)

    @pl.when(j == 0)
    def _():
        h_ref[...] = _modulate(x_ref[...], g_ref[...], sh_ref[...], sc_ref[...]).astype(BF16)
        acc_ref[...] = jnp.zeros_like(acc_ref)

    h = h_ref[...]
    a = jnp.dot(h, w1_ref[...], preferred_element_type=F32)
    b = jnp.dot(h, w3_ref[...], preferred_element_type=F32)
    acc_ref[...] += jnp.dot((_silu(a) * b).astype(BF16), w2_ref[...], preferred_element_type=F32)

    @pl.when(j == pl.num_programs(1) - 1)
    def _():
        y = x_ref[...] + (0.5 * gt_ref[...]) * acc_ref[...]
        if final:
            ms = jnp.mean(y * y, axis=-1, keepdims=True)
            y = (y * lax.rsqrt(ms + EPS)) * fg_ref[...]
        o_ref[...] = y


def _ffn(x, mod_l, kinds, slot0, rows_per_slot, norm_g, w1, w3, w2, final_g=None):
    m, d = x.shape
    f = w1.shape[1]
    tm = _pick(rows_per_slot, 512)
    tn = _pick(f, 512)
    final = final_g is not None
    k_sh, k_sc, k_gt = kinds
    in_specs = [
        pl.BlockSpec((tm, d), lambda i, j: (i, 0)),
        _mod_spec(k_sh, slot0, rows_per_slot, tm, d),
        _mod_spec(k_sc, slot0, rows_per_slot, tm, d),
        _mod_spec(k_gt, slot0, rows_per_slot, tm, d),
        pl.BlockSpec((1, d), lambda i, j: (0, 0)),
        pl.BlockSpec((d, tn), lambda i, j: (0, j)),
        pl.BlockSpec((d, tn), lambda i, j: (0, j)),
        pl.BlockSpec((tn, d), lambda i, j: (j, 0)),
    ]
    args = [x, mod_l, mod_l, mod_l, norm_g.reshape(1, d), w1, w3, w2]
    if final:
        in_specs.append(pl.BlockSpec((1, d), lambda i, j: (0, 0)))
        args.append(final_g.reshape(1, d))
    return pl.pallas_call(
        functools.partial(_ffn_kernel, final=final),
        grid=(m // tm, f // tn),
        in_specs=in_specs,
        out_specs=pl.BlockSpec((tm, d), lambda i, j: (i, 0)),
        out_shape=jax.ShapeDtypeStruct((m, d), F32),
        scratch_shapes=[pltpu.VMEM((tm, d), BF16), pltpu.VMEM((tm, d), F32)],
        compiler_params=_params(("parallel", "arbitrary")),
        name="ffn",
    )(*args)


def _rope_tile(res, cos, sin_a, sin_b):
    outs = []
    for k in range(res.shape[1] // HEAD_DIM):
        r = res[:, k * HEAD_DIM:(k + 1) * HEAD_DIM]
        up = pltpu.roll(r, HEAD_DIM - HEAD_DIM // 4, 1)
        dn = pltpu.roll(r, HEAD_DIM // 4, 1)
        outs.append(r * cos + up * sin_a + dn * sin_b)
    return jnp.concatenate(outs, axis=1)


def _proj_kernel(*refs, rope_tiles):
    if rope_tiles:
        x_ref, sh_ref, sc_ref, g_ref, w_ref, cos_ref, sa_ref, sb_ref, o_ref, h_ref = refs
    else:
        x_ref, sh_ref, sc_ref, g_ref, w_ref, o_ref, h_ref = refs
    j = pl.program_id(1)

    @pl.when(j == 0)
    def _():
        h_ref[...] = _modulate(x_ref[...], g_ref[...], sh_ref[...], sc_ref[...]).astype(BF16)

    res = jnp.dot(h_ref[...], w_ref[...], preferred_element_type=F32)
    if rope_tiles:
        is_rope = functools.reduce(jnp.logical_or, [j == t for t in rope_tiles])

        @pl.when(is_rope)
        def _():
            o_ref[...] = _rope_tile(res, cos_ref[...], sa_ref[...], sb_ref[...]).astype(o_ref.dtype)

        @pl.when(jnp.logical_not(is_rope))
        def _():
            o_ref[...] = res.astype(o_ref.dtype)
    else:
        o_ref[...] = res.astype(o_ref.dtype)


def _proj(x, mod_l, kinds, slot0, rows_per_slot, norm_g, w, out_dtype, tn_want, rope=None):
    m, d = x.shape
    n = w.shape[1]
    tm = _pick(rows_per_slot, 1024)
    tn = _pick(n, tn_want)
    k_sh, k_sc = kinds
    in_specs = [
        pl.BlockSpec((tm, d), lambda i, j: (i, 0)),
        _mod_spec(k_sh, slot0, rows_per_slot, tm, d),
        _mod_spec(k_sc, slot0, rows_per_slot, tm, d),
        pl.BlockSpec((1, d), lambda i, j: (0, 0)),
        pl.BlockSpec((d, tn), lambda i, j: (0, j)),
    ]
    args = [x, mod_l, mod_l, norm_g.reshape(1, d), w]
    rope_tiles = ()
    if rope is not None:
        tables, rope_cols = rope
        assert rope_cols[0] % tn == 0 and rope_cols[1] % tn == 0
        rope_tiles = tuple(range(rope_cols[0] // tn, rope_cols[1] // tn))
        tps = rows_per_slot // tm
        for t in tables:
            in_specs.append(pl.BlockSpec((tm, HEAD_DIM), lambda i, j: (i % tps, 0)))
            args.append(t)
    return pl.pallas_call(
        functools.partial(_proj_kernel, rope_tiles=rope_tiles),
        grid=(m // tm, n // tn),
        in_specs=in_specs,
        out_specs=pl.BlockSpec((tm, tn), lambda i, j: (i, j)),
        out_shape=jax.ShapeDtypeStruct((m, n), out_dtype),
        scratch_shapes=[pltpu.VMEM((tm, d), BF16)],
        compiler_params=_params(("parallel", "arbitrary")),
        name="proj",
    )(*args)


def _proj_conv_kernel(xp_ref, x_ref, xn_ref, sh_ref, sc_ref, g_ref, w_ref, cw_ref, cb_ref, o_ref, h_ref,
                      *, tm, n_plain, tiles_per_seq):
    i = pl.program_id(0)
    j = pl.program_id(1)
    n_all = tm + 2 * CONV_HALO

    @pl.when(j == 0)
    def _():
        g, sh, sc = g_ref[...], sh_ref[...], sc_ref[...]
        t = i % tiles_per_seq
        keep_prev = (t != 0).astype(F32)
        keep_next = (t != tiles_per_seq - 1).astype(F32)
        h_ref[0:CONV_HALO, :] = (_modulate(xp_ref[...], g, sh, sc) * keep_prev).astype(BF16)
        h_ref[CONV_HALO:CONV_HALO + tm, :] = _modulate(x_ref[...], g, sh, sc).astype(BF16)
        h_ref[CONV_HALO + tm:n_all, :] = (_modulate(xn_ref[...], g, sh, sc) * keep_next).astype(BF16)

    @pl.when(j < n_plain)
    def _():
        o_ref[...] = jnp.dot(h_ref[CONV_HALO:CONV_HALO + tm, :], w_ref[...],
                             preferred_element_type=F32).astype(o_ref.dtype)

    @pl.when(j >= n_plain)
    def _():
        w = w_ref[...]
        step = n_all // CONV_ROW_BLOCKS if n_all % (CONV_ROW_BLOCKS * BF16_SUBLANES) == 0 else n_all
        u = jnp.concatenate([jnp.dot(h_ref[r0:r0 + step, :], w, preferred_element_type=F32)
                             for r0 in range(0, n_all, step)], axis=0)
        left = (SSM_CONV - 1) // 2
        acc = cb_ref[...] + cw_ref[left:left + 1, :] * u
        for tap in range(SSM_CONV):
            if tap == left:
                continue
            acc = acc + cw_ref[tap:tap + 1, :] * pltpu.roll(u, (left - tap) % n_all, 0)
        y = acc[CONV_HALO:CONV_HALO + tm, :]
        o_ref[...] = _silu(y).astype(o_ref.dtype)


def _proj_conv(x, mod_l, kinds, slot0, rows_per_slot, rows_per_seq, norm_g, w, n_plain_cols, conv_w, conv_b):
    m, d = x.shape
    n = w.shape[1]
    tm = _pick(rows_per_seq, 1024)
    tn = _pick(math.gcd(n_plain_cols, n - n_plain_cols), 512)
    n_plain = n_plain_cols // tn
    tps = rows_per_seq // tm
    k_sh, k_sc = kinds
    hb = tm // CONV_HALO
    last_hb = m // CONV_HALO - 1
    in_specs = [
        pl.BlockSpec((CONV_HALO, d), lambda i, j: (jnp.maximum(i * hb - 1, 0), 0)),
        pl.BlockSpec((tm, d), lambda i, j: (i, 0)),
        pl.BlockSpec((CONV_HALO, d), lambda i, j: (jnp.minimum((i + 1) * hb, last_hb), 0)),
        _mod_spec(k_sh, slot0, rows_per_slot, tm, d),
        _mod_spec(k_sc, slot0, rows_per_slot, tm, d),
        pl.BlockSpec((1, d), lambda i, j: (0, 0)),
        pl.BlockSpec((d, tn), lambda i, j: (0, j)),
        pl.BlockSpec((SSM_CONV, tn), lambda i, j: (0, jnp.maximum(j - n_plain, 0))),
        pl.BlockSpec((1, tn), lambda i, j: (0, jnp.maximum(j - n_plain, 0))),
    ]
    return pl.pallas_call(
        functools.partial(_proj_conv_kernel, tm=tm, n_plain=n_plain, tiles_per_seq=tps),
        grid=(m // tm, n // tn),
        in_specs=in_specs,
        out_specs=pl.BlockSpec((tm, tn), lambda i, j: (i, j)),
        out_shape=jax.ShapeDtypeStruct((m, n), BF16),
        scratch_shapes=[pltpu.VMEM((tm + 2 * CONV_HALO, d), BF16)],
        compiler_params=_params(("parallel", "arbitrary")),
        name="proj_conv",
    )(x, x, x, mod_l, mod_l, norm_g.reshape(1, d), w, conv_w, conv_b.reshape(1, -1))


def _outproj_kernel(*refs, n_a):
    x_ref, gt_ref = refs[0], refs[1]
    a_refs = refs[2:2 + n_a]
    w_refs = refs[2 + n_a:2 + 2 * n_a]
    o_ref = refs[2 + 2 * n_a]
    acc = jnp.dot(a_refs[0][...], w_refs[0][...], preferred_element_type=F32)
    for a_ref, w_ref in zip(a_refs[1:], w_refs[1:]):
        acc = acc + jnp.dot(a_ref[...], w_ref[...], preferred_element_type=F32)
    o_ref[...] = x_ref[...] + gt_ref[...] * acc


def _outproj(x, mod_l, k_gt, slot0, rows_per_slot, a_list, w):
    m, d = x.shape
    tm = _pick(rows_per_slot, 1024)
    tn = _pick(d, 512)
    in_specs = [pl.BlockSpec((tm, tn), lambda i, j: (i, j)), _mod_spec(k_gt, slot0, rows_per_slot, tm, tn, col=True)]
    args = [x, mod_l]
    k0 = 0
    w_specs = []
    for a in a_list:
        ka = a.shape[1]
        assert k0 % ka == 0
        in_specs.append(pl.BlockSpec((tm, ka), lambda i, j: (i, 0)))
        w_specs.append(pl.BlockSpec((ka, tn), lambda i, j, kb=k0 // ka: (kb, j)))
        k0 += ka
    assert k0 == w.shape[0]
    return pl.pallas_call(
        functools.partial(_outproj_kernel, n_a=len(a_list)),
        grid=(m // tm, d // tn),
        in_specs=in_specs + w_specs,
        out_specs=pl.BlockSpec((tm, tn), lambda i, j: (i, j)),
        out_shape=jax.ShapeDtypeStruct((m, d), F32),
        compiler_params=_params(("parallel", "parallel")),
        name="outproj",
    )(*args, *a_list, *([w] * len(a_list)))


def _na_plan(rows):
    kr = min(NA_WIN_ROWS, rows)
    qr = NA_QROWS if rows % NA_QROWS == 0 else 1
    slab = min(rows, qr + kr)
    starts, ids, layouts = [], [], []
    for b in range(rows // qr):
        ks = int(np.clip(qr * b - kr // 2, 0, rows - slab))
        lay = []
        for r in range(qr * b, qr * b + qr):
            rs = int(np.clip(r - kr // 2, 0, rows - kr))
            assert ks <= rs and rs + kr <= ks + slab
            lay.append((rs - ks, r - ks))
        lay = tuple(lay)
        if lay not in layouts:
            layouts.append(lay)
        starts.append(ks)
        ids.append(layouts.index(lay))
    return kr, qr, slab, np.array([starts, ids], np.int32), tuple(layouts)


def _na_bias_kernel(rpb_ref, o_ref, *, kr, slab, layouts):
    h = pl.program_id(0)
    n_ro, n_co = 2 * NA_WIN_ROWS - 1, 2 * NA_WIN_COLS - 1
    qc = lax.broadcasted_iota(jnp.int32, (GRID_W, GRID_W), 0)
    kc = lax.broadcasted_iota(jnp.int32, (GRID_W, GRID_W), 1)
    dcol = kc - qc + (NA_WIN_COLS - 1)
    cs = jnp.clip(qc - NA_WIN_COLS // 2, 0, GRID_W - NA_WIN_COLS)
    valid = jnp.logical_and(kc >= cs, kc < cs + NA_WIN_COLS)
    tabs = []
    for ro in range(n_ro):
        acc = jnp.zeros((GRID_W, GRID_W), F32)
        for dd in range(n_co):
            acc = jnp.where(dcol == dd, rpb_ref[(h * n_ro + ro) * n_co + dd], acc)
        tabs.append(jnp.where(valid, acc * LOG2E, NEG_BIG))
    masked = jnp.full((GRID_W, GRID_W), NEG_BIG, F32)
    for ti, lay in enumerate(layouts):
        for qi, (win0, qrow) in enumerate(lay):
            for ki in range(slab):
                ro = ki - qrow + (NA_WIN_ROWS - 1)
                tile = tabs[ro] if win0 <= ki < win0 + kr else masked
                o_ref[ti, qi * GRID_W:(qi + 1) * GRID_W, ki * GRID_W:(ki + 1) * GRID_W] = tile


def _na_bias(rpb, plan):
    kr, qr, slab, _, layouts = plan
    nh = rpb.shape[0]
    shape = (len(layouts), qr * GRID_W, slab * GRID_W)
    return pl.pallas_call(
        functools.partial(_na_bias_kernel, kr=kr, slab=slab, layouts=layouts),
        grid=(nh,),
        in_specs=[pl.BlockSpec(memory_space=pltpu.SMEM)],
        out_specs=pl.BlockSpec((None,) + shape, lambda h: (h, 0, 0, 0)),
        out_shape=jax.ShapeDtypeStruct((nh,) + shape, F32),
        compiler_params=_params(("parallel",)),
        name="na_bias",
    )(rpb.reshape(-1))


def _dot_t(a, b):
    return lax.dot_general(a, b, (((1,), (1,)), ((), ())), preferred_element_type=F32)


def _na_kernel(plan_ref, q_ref, k_ref, v_ref, kc_ref, vc_ref, bias_ref, o_ref, *, n_blocks, nq, nk):
    c = HEAD_DIM ** -0.5 * LOG2E
    kc = kc_ref[...]
    vc = vc_ref[...]

    def block(blk, carry):
        q0 = pl.multiple_of(blk * nq, nq)
        k0 = pl.multiple_of(plan_ref[0, blk] * GRID_W, GRID_W)
        q = q_ref[pl.ds(q0, nq), :]
        s = _dot_t(q, k_ref[pl.ds(k0, nk), :]) * c + bias_ref[plan_ref[1, blk]]
        sc = _dot_t(q, kc) * c
        m = jnp.maximum(jnp.max(s, axis=-1, keepdims=True), jnp.max(sc, axis=-1, keepdims=True))
        p = jnp.exp2(s - m)
        pc = jnp.exp2(sc - m)
        inv = 1.0 / (jnp.sum(p, axis=-1, keepdims=True) + jnp.sum(pc, axis=-1, keepdims=True))
        o = (jnp.dot(p.astype(BF16), v_ref[pl.ds(k0, nk), :], preferred_element_type=F32)
             + jnp.dot(pc.astype(BF16), vc, preferred_element_type=F32)) * inv
        o_ref[pl.ds(q0, nq), :] = o.astype(o_ref.dtype)
        return carry

    lax.fori_loop(0, n_blocks, block, 0)


def _na_attention(qkv_x, qkv_c, bias, plan, batch, seq, ctx_len, n_heads):
    _, qr, slab, plan_arr, layouts = plan
    tab = (len(layouts), qr * GRID_W, slab * GRID_W)
    return pl.pallas_call(
        functools.partial(_na_kernel, n_blocks=plan_arr.shape[1], nq=qr * GRID_W, nk=slab * GRID_W),
        grid=(n_heads, batch),
        in_specs=[
            pl.BlockSpec(memory_space=pltpu.SMEM),
            pl.BlockSpec((seq, HEAD_DIM), lambda h, b: (b, h)),
            pl.BlockSpec((seq, HEAD_DIM), lambda h, b: (b, n_heads + h)),
            pl.BlockSpec((seq, HEAD_DIM), lambda h, b: (b, 2 * n_heads + h)),
            pl.BlockSpec((ctx_len, HEAD_DIM), lambda h, b: (b, n_heads + h)),
            pl.BlockSpec((ctx_len, HEAD_DIM), lambda h, b: (b, 2 * n_heads + h)),
            pl.BlockSpec((None,) + tab, lambda h, b: (h, 0, 0, 0)),
        ],
        out_specs=pl.BlockSpec((seq, HEAD_DIM), lambda h, b: (b, h)),
        out_shape=jax.ShapeDtypeStruct((batch * seq, n_heads * HEAD_DIM), BF16),
        compiler_params=_params(("parallel", "parallel")),
        name="na_attn",
    )(jnp.asarray(plan_arr), qkv_x, qkv_x, qkv_x, qkv_c, qkv_c, bias)


def _ctx_attn_kernel(q_ref, k_ref, v_ref, o_ref):
    s = _dot_t(q_ref[...], k_ref[...]) * HEAD_DIM ** -0.5
    p = jnp.exp(s - jnp.max(s, axis=-1, keepdims=True))
    p = p * (1.0 / jnp.sum(p, axis=-1, keepdims=True))
    o_ref[...] = jnp.dot(p.astype(BF16), v_ref[...], preferred_element_type=F32).astype(o_ref.dtype)


def _ctx_attention(qkv_c, batch, ctx_len, n_heads):
    return pl.pallas_call(
        _ctx_attn_kernel,
        grid=(batch, n_heads),
        in_specs=[
            pl.BlockSpec((ctx_len, HEAD_DIM), lambda b, h: (b, h)),
            pl.BlockSpec((ctx_len, HEAD_DIM), lambda b, h: (b, n_heads + h)),
            pl.BlockSpec((ctx_len, HEAD_DIM), lambda b, h: (b, 2 * n_heads + h)),
        ],
        out_specs=pl.BlockSpec((ctx_len, HEAD_DIM), lambda b, h: (b, h)),
        out_shape=jax.ShapeDtypeStruct((batch * ctx_len, n_heads * HEAD_DIM), BF16),
        compiler_params=_params(("parallel", "parallel")),
        name="ctx_attn",
    )(qkv_c, qkv_c, qkv_c)


def _diff_kernel(*refs, n_src, lam_init):
    lam_ref, g_ref, q_ref = refs[0], refs[1], refs[2]
    k_refs = refs[3:3 + n_src]
    v_refs = refs[3 + n_src:3 + 2 * n_src]
    o_ref = refs[3 + 2 * n_src]
    c = HEAD_DIM ** -0.5 * LOG2E
    lf = lam_ref[...]
    lam = (jnp.exp(jnp.sum(lf[0:1] * lf[1:2], axis=-1, keepdims=True))
           - jnp.exp(jnp.sum(lf[2:3] * lf[3:4], axis=-1, keepdims=True)) + lam_init)

    def probs(mi):
        qm = q_ref[:, mi * HEAD_DIM:(mi + 1) * HEAD_DIM]
        ss = [_dot_t(qm, k_ref[:, mi * HEAD_DIM:(mi + 1) * HEAD_DIM]) for k_ref in k_refs]
        m = functools.reduce(jnp.maximum, [jnp.max(s, axis=-1, keepdims=True) for s in ss])
        ps = [jnp.exp2((s - m) * c) for s in ss]
        l = functools.reduce(jnp.add, [jnp.sum(p, axis=-1, keepdims=True) for p in ps])
        return ps, 1.0 / l

    p1, i1 = probs(0)
    p2, i2 = probs(1)
    c2 = lam * i2
    o = None
    for pa, pb, v_ref in zip(p1, p2, v_refs):
        pd = (pa * i1 - pb * c2).astype(BF16)
        t = jnp.dot(pd, v_ref[...], preferred_element_type=F32)
        o = t if o is None else o + t
    ms = jnp.mean(o * o, axis=-1, keepdims=True)
    o_ref[...] = ((o * lax.rsqrt(ms + EPS)) * g_ref[...] * (1.0 - lam_init)).astype(o_ref.dtype)


def _diff_attention(q_arr, q_rows, kv_list, lam_vecs, subln_g, lam_init, batch, n_heads, q_col0, tq_want):
    hw = 2 * HEAD_DIM
    tq = _pick(q_rows, tq_want)
    nq = q_rows // tq
    qb = q_col0 // hw
    in_specs = [
        pl.BlockSpec((4, HEAD_DIM), lambda b, h, i: (0, 0)),
        pl.BlockSpec((1, hw), lambda b, h, i: (0, 0)),
        pl.BlockSpec((tq, hw), lambda b, h, i: (b * nq + i, qb + h)),
    ]
    args = [lam_vecs, subln_g.reshape(1, hw), q_arr]
    for arr, rows in kv_list:
        in_specs.append(pl.BlockSpec((rows, hw), lambda b, h, i: (b, qb + n_heads + h)))
        args.append(arr)
    for arr, rows in kv_list:
        in_specs.append(pl.BlockSpec((rows, hw), lambda b, h, i: (b, qb + 2 * n_heads + h)))
        args.append(arr)
    return pl.pallas_call(
        functools.partial(_diff_kernel, n_src=len(kv_list), lam_init=lam_init),
        grid=(batch, n_heads, nq),
        in_specs=in_specs,
        out_specs=pl.BlockSpec((tq, hw), lambda b, h, i: (b * nq + i, h)),
        out_shape=jax.ShapeDtypeStruct((batch * q_rows, n_heads * hw), BF16),
        compiler_params=_params(("parallel", "parallel", "parallel")),
        name="diff_attn",
    )(*args)


def _cumsum_rows(a, reverse):
    n = a.shape[0]
    row = lax.broadcasted_iota(jnp.int32, a.shape, 0)
    d = 1
    while d < n:
        if reverse:
            a = a + jnp.where(row < n - d, pltpu.roll(a, n - d, 0), 0.0)
        else:
            a = a + jnp.where(row >= d, pltpu.roll(a, d, 0), 0.0)
        d *= 2
    return a


def _ssd_kernel(*refs, n_ctx_chunks, n_x_chunks, hpg, n_heads, ctx_out):
    (xs_x, z_x, b_x, c_x, dt_x, xs_c, z_c, b_c, c_c, dt_c, bias_ref, alog_ref, dsum_ref, ng_ref) = refs[:14]
    n_out = 2 if ctx_out else 1
    yx_ref = refs[14]
    yc_ref = refs[15] if ctx_out else None
    yacc_x, yacc_c, sc_x, sc_c, rft_x, rft_c, hf_ref, hb_ref = refs[14 + n_out:]
    t = SSM_CHUNK
    pw = 2 * SSM_HEADDIM
    g = pl.program_id(1)
    nl = 2 * n_heads

    lane = lax.broadcasted_iota(jnp.int32, (t, nl), 1)
    is_fwd = lane < n_heads
    bias = bias_ref[...]
    neg_a2 = -jnp.exp(alog_ref[...]) * LOG2E
    li = lax.broadcasted_iota(jnp.int32, (t, t), 0)
    si = lax.broadcasted_iota(jnp.int32, (t, t), 1)
    add_lo = jnp.where(li >= si, 0.0, NEG_BIG)
    add_up = jnp.where(li <= si, 0.0, NEG_BIG)
    pair_lane = lax.broadcasted_iota(jnp.int32, (t, pw), 1) < SSM_HEADDIM
    lane_shift = (nl - g * hpg) % nl

    def pair_weights(xp):
        zero = jnp.zeros_like(xp)
        return jnp.concatenate([jnp.where(pair_lane, xp, zero), jnp.where(pair_lane, zero, xp)], axis=0)

    def col(scl, idx):
        return jnp.broadcast_to(scl[:, idx:idx + 1], (t, t))

    def fwd_chunk(xs_ref, b_ref, c_ref, dt_ref, yacc, sc_ref, rft_ref, r0, h0):
        rows = pl.ds(r0, t)
        dt = jax.nn.softplus(dt_ref[rows, :] + bias)
        a2 = dt * neg_a2
        sc2 = jnp.where(is_fwd, _cumsum_rows(a2, False), _cumsum_rows(a2, True))
        sc_ref[rows, :] = sc2
        rft_ref[pl.ds(h0, nl), :] = (sc2 - jnp.log(dt) * LOG2E).T
        scl = pltpu.roll(sc2, lane_shift, 1)
        xb = xs_ref[rows, :]
        bb = b_ref[rows, :]
        cb_ = c_ref[rows, :]
        cbm = _dot_t(cb_, bb)
        bt = bb.astype(F32).T
        yoff = jnp.dot(cb_, hf_ref[...].astype(BF16), preferred_element_type=F32)
        for jp in range(hpg // 2):
            ms, bws, es = [], [], []
            for j in (2 * jp, 2 * jp + 1):
                cf = col(scl, j)
                cbk = col(scl, n_heads + j)
                rf = rft_ref[pl.ds(h0 + g * hpg + j, 1), :]
                rb = rft_ref[pl.ds(h0 + n_heads + g * hpg + j, 1), :]
                ms.append((cbm * (jnp.exp2(cf - rf + add_lo) + jnp.exp2(cbk - rb + add_up))).astype(BF16))
                bws.append((bt * jnp.exp2(cf[t - 1:t, :] - rf)).astype(BF16))
                es.append(jnp.exp2(cf))
            cols = slice(jp * pw, (jp + 1) * pw)
            xp = xb[:, cols]
            lhs = jnp.concatenate([jnp.concatenate(ms, axis=1), jnp.concatenate(bws, axis=1)], axis=0)
            res = jnp.dot(lhs, pair_weights(xp), preferred_element_type=F32)
            e_pair = jnp.where(pair_lane, es[0], es[1])
            yacc[rows, cols] = res[:t] + xp.astype(F32) * dsum_ref[:, cols] + yoff[:, cols] * e_pair
            hf_ref[:, cols] = hf_ref[:, cols] * e_pair[t - 1:t, :] + res[t:]

    def bwd_chunk(xs_ref, z_ref, b_ref, c_ref, yacc, sc_ref, rft_ref, y_ref, r0, h0):
        rows = pl.ds(r0, t)
        scl = pltpu.roll(sc_ref[rows, :], lane_shift, 1)
        xb = xs_ref[rows, :]
        bt = b_ref[rows, :].astype(F32).T
        yoff = jnp.dot(c_ref[rows, :], hb_ref[...].astype(BF16), preferred_element_type=F32)
        ys = []
        for jp in range(hpg // 2):
            bws, es = [], []
            for j in (2 * jp, 2 * jp + 1):
                cbk = col(scl, n_heads + j)
                rb = rft_ref[pl.ds(h0 + n_heads + g * hpg + j, 1), :]
                bws.append((bt * jnp.exp2(cbk[0:1, :] - rb)).astype(BF16))
                es.append(jnp.exp2(cbk))
            cols = slice(jp * pw, (jp + 1) * pw)
            grow = jnp.dot(jnp.concatenate(bws, axis=1), pair_weights(xb[:, cols]), preferred_element_type=F32)
            e_pair = jnp.where(pair_lane, es[0], es[1])
            ys.append(yacc[rows, cols] + yoff[:, cols] * e_pair)
            hb_ref[:, cols] = hb_ref[:, cols] * e_pair[0:1, :] + grow
        if y_ref is not None:
            yg = jnp.concatenate(ys, axis=1) * _silu(z_ref[rows, :].astype(F32))
            ms = jnp.mean(yg * yg, axis=-1, keepdims=True)
            y_ref[rows, :] = ((yg * lax.rsqrt(ms + EPS)) * ng_ref[...]).astype(y_ref.dtype)

    hf_ref[...] = jnp.zeros_like(hf_ref)
    hb_ref[...] = jnp.zeros_like(hb_ref)
    for k in range(n_ctx_chunks):
        fwd_chunk(xs_c, b_c, c_c, dt_c, yacc_c, sc_c, rft_c, k * t, k * nl)

    def fx(k, carry):
        fwd_chunk(xs_x, b_x, c_x, dt_x, yacc_x, sc_x, rft_x, pl.multiple_of(k * t, t), pl.multiple_of(k * nl, nl))
        return carry

    lax.fori_loop(0, n_x_chunks, fx, 0, unroll=2)
    for k in reversed(range(n_ctx_chunks)):
        bwd_chunk(xs_c, z_c, b_c, c_c, yacc_c, sc_c, rft_c, yc_ref, k * t, k * nl)

    def bx(k, carry):
        kk = n_x_chunks - 1 - k
        bwd_chunk(xs_x, z_x, b_x, c_x, yacc_x, sc_x, rft_x, yx_ref, pl.multiple_of(kk * t, t),
                  pl.multiple_of(kk * nl, nl))
        return carry

    lax.fori_loop(0, n_x_chunks, bx, 0, unroll=2)


def _ssd(zx, zc, dtx, dtc, a_log, dt_bias, d_skip, norm_g, batch, seq, ctx_len, ctx_out):
    inner = norm_g.shape[0]
    n_heads = inner // SSM_HEADDIM
    hpg = n_heads // SSM_GROUPS
    gw = hpg * SSM_HEADDIM
    nl = 2 * n_heads
    assert hpg % 2 == 0 and seq % SSM_CHUNK == 0 and ctx_len % SSM_CHUNK == 0
    assert SSM_CHUNK == 2 * SSM_HEADDIM == SSM_STATE == V7X_LANES and nl == V7X_LANES
    zb = inner // gw
    b0 = (2 * inner) // SSM_STATE
    c0 = b0 + SSM_GROUPS

    def specs(rows):
        return [
            pl.BlockSpec((rows, gw), lambda b, g: (b, zb + g)),
            pl.BlockSpec((rows, gw), lambda b, g: (b, g)),
            pl.BlockSpec((rows, SSM_STATE), lambda b, g: (b, b0 + g)),
            pl.BlockSpec((rows, SSM_STATE), lambda b, g: (b, c0 + g)),
            pl.BlockSpec((rows, nl), lambda b, g: (b, 0)),
        ]

    dsum = jnp.repeat(d_skip[0] + d_skip[1], SSM_HEADDIM).reshape(1, inner).astype(F32)
    in_specs = specs(seq) + specs(ctx_len) + [
        pl.BlockSpec((1, nl), lambda b, g: (0, 0)),
        pl.BlockSpec((1, nl), lambda b, g: (0, 0)),
        pl.BlockSpec((1, gw), lambda b, g: (0, g)),
        pl.BlockSpec((1, gw), lambda b, g: (0, g)),
    ]
    out_specs = [pl.BlockSpec((seq, gw), lambda b, g: (b, g))]
    out_shape = [jax.ShapeDtypeStruct((batch * seq, inner), BF16)]
    if ctx_out:
        out_specs.append(pl.BlockSpec((ctx_len, gw), lambda b, g: (b, g)))
        out_shape.append(jax.ShapeDtypeStruct((batch * ctx_len, inner), BF16))
    ncx, ncc = seq // SSM_CHUNK, ctx_len // SSM_CHUNK
    scratch = [
        pltpu.VMEM((seq, gw), F32), pltpu.VMEM((ctx_len, gw), F32),
        pltpu.VMEM((seq, nl), F32), pltpu.VMEM((ctx_len, nl), F32),
        pltpu.VMEM((ncx * nl, SSM_CHUNK), F32), pltpu.VMEM((ncc * nl, SSM_CHUNK), F32),
        pltpu.VMEM((SSM_STATE, gw), F32), pltpu.VMEM((SSM_STATE, gw), F32),
    ]
    outs = pl.pallas_call(
        functools.partial(_ssd_kernel, n_ctx_chunks=ctx_len // SSM_CHUNK, n_x_chunks=seq // SSM_CHUNK,
                          hpg=hpg, n_heads=n_heads, ctx_out=ctx_out),
        grid=(batch, SSM_GROUPS),
        in_specs=in_specs,
        out_specs=out_specs,
        out_shape=out_shape,
        scratch_shapes=scratch,
        compiler_params=_params(("parallel", "parallel")),
        name="ssd",
    )(zx, zx, zx, zx, dtx, zc, zc, zc, zc, dtc,
      dt_bias.reshape(1, nl), a_log.reshape(1, nl), dsum, norm_g.reshape(1, inner))
    return (outs[0], outs[1]) if ctx_out else (outs[0], None)


def _rope_tables(seq):
    quarter = HEAD_DIM // 4
    inv = 1.0 / (ROPE_THETA ** (jnp.arange(quarter, dtype=F32) / quarter))
    t = jnp.arange(seq)
    row = (t // GRID_W).astype(F32)[:, None] * inv
    col = (t % GRID_W).astype(F32)[:, None] * inv
    ang = jnp.concatenate([row, row, col, col], axis=-1)
    cos, sin = jnp.cos(ang), jnp.sin(ang)
    first = (np.arange(HEAD_DIM) % (2 * quarter)) < quarter
    return cos, jnp.where(first, -sin, 0.0), jnp.where(first, 0.0, sin)


def kernel(x, c, ctx, c_ctx, mod_w, mod_b, norm_g, ffn_w1, ffn_w3, ffn_w2, attn_w_in, attn_w_out, na_rpb,
           diff_lambda, diff_subln_g, ssm_w_in, ssm_conv_w, ssm_conv_b, ssm_a_log, ssm_dt_bias, ssm_d,
           ssm_norm_g, ssm_w_out, final_norm_g):
    batch, seq, d = x.shape
    ctx_len = ctx.shape[1]
    depth = mod_w.shape[0]
    assert batch + 1 <= MOD_SLOTS
    na_w = d // 2
    na_heads = na_w // HEAD_DIM
    diff_heads = na_w // (2 * HEAD_DIM)
    inner = ssm_norm_g.shape[1]
    n_ssm_heads = inner // SSM_HEADDIM

    cvec = jnp.zeros((MOD_SLOTS, d), F32).at[:batch].set(c).at[batch].set(c_ctx)
    mod = _modulation(cvec, mod_w, mod_b)
    mod = mod.reshape(depth, MOD_SLOTS, N_MOD, d).transpose(0, 2, 1, 3)[:, :, :, None, :]

    xs = x.reshape(batch * seq, d)
    hs = ctx.reshape(batch * ctx_len, d)
    rope = _rope_tables(seq)
    na_plan = _na_plan(seq // GRID_W)
    xslot = (0, seq)
    cslot = (batch, batch * ctx_len)

    for layer in range(depth):
        ctx_out = layer < depth - 1
        last = layer == depth - 1
        ml = mod[layer]
        ng = norm_g[layer]
        w1a, w3a, w2a = (w[layer, 0].astype(BF16) for w in (ffn_w1, ffn_w3, ffn_w2))
        w1b, w3b, w2b = (w[layer, 1].astype(BF16) for w in (ffn_w1, ffn_w3, ffn_w2))

        xs = _ffn(xs, ml, (0, 1, 2), *xslot, ng[0], w1a, w3a, w2a)
        hs = _ffn(hs, ml, (0, 1, 2), *cslot, ng[0], w1a, w3a, w2a)

        i = layer // 2
        if layer % 2 == 0:
            lam_init = 0.8 - 0.6 * math.exp(-0.3 * layer)
            w_in = attn_w_in[i].astype(BF16)
            w_out = attn_w_out[i].astype(BF16)
            dq0 = 3 * na_w
            qkv_x = _proj(xs, ml, (3, 4), *xslot, ng[1], w_in, BF16, 1024, rope=(rope, (dq0, dq0 + 2 * na_w)))
            qkv_c = _proj(hs, ml, (3, 4), *cslot, ng[1], w_in, BF16, 1024)
            bias = _na_bias(na_rpb[i], na_plan)
            na_x = _na_attention(qkv_x, qkv_c, bias, na_plan, batch, seq, ctx_len, na_heads)
            df_x = _diff_attention(qkv_x, seq, [(qkv_x, seq), (qkv_c, ctx_len)], diff_lambda[i], diff_subln_g[i],
                                   lam_init, batch, diff_heads, dq0, 256)
            xs = _outproj(xs, ml, 5, *xslot, [na_x, df_x], w_out)
            if ctx_out:
                na_c = _ctx_attention(qkv_c, batch, ctx_len, na_heads)
                df_c = _diff_attention(qkv_c, ctx_len, [(qkv_c, ctx_len)], diff_lambda[i], diff_subln_g[i],
                                       lam_init, batch, diff_heads, dq0, 256)
                hs = _outproj(hs, ml, 5, *cslot, [na_c, df_c], w_out)
        else:
            n_zx = 2 * inner + 2 * SSM_GROUPS * SSM_STATE
            w_in = ssm_w_in[i]
            w_main = w_in[:, :n_zx].astype(BF16)
            w_dt = w_in[:, n_zx:].astype(BF16)
            cw, cb = ssm_conv_w[i], ssm_conv_b[i]
            zx = _proj_conv(xs, ml, (3, 4), *xslot, seq, ng[1], w_main, inner, cw, cb)
            zc = _proj_conv(hs, ml, (3, 4), *cslot, ctx_len, ng[1], w_main, inner, cw, cb)
            dtx = _proj(xs, ml, (3, 4), *xslot, ng[1], w_dt, F32, 2 * n_ssm_heads)
            dtc = _proj(hs, ml, (3, 4), *cslot, ng[1], w_dt, F32, 2 * n_ssm_heads)
            yx, yc = _ssd(zx, zc, dtx, dtc, ssm_a_log[i], ssm_dt_bias[i], ssm_d[i], ssm_norm_g[i],
                          batch, seq, ctx_len, ctx_out)
            w_out = ssm_w_out[i].astype(BF16)
            xs = _outproj(xs, ml, 5, *xslot, [yx], w_out)
            if ctx_out:
                hs = _outproj(hs, ml, 5, *cslot, [yc], w_out)

        xs = _ffn(xs, ml, (6, 7, 8), *xslot, ng[2], w1b, w3b, w2b, final_g=final_norm_g if last else None)
        if ctx_out:
            hs = _ffn(hs, ml, (6, 7, 8), *cslot, ng[2], w1b, w3b, w2b)
    return xs.reshape(batch, seq, d)
```

```python
import functools
import math

import jax
import jax.numpy as jnp
import numpy as np
from jax import lax
from jax.experimental import pallas as pl
from jax.experimental.pallas import tpu as pltpu

F32 = jnp.float32
BF16 = jnp.bfloat16

HEAD_DIM = 128
GRID_W = 64
N_MOD = 9
NA_WIN_ROWS = 8
NA_WIN_COLS = 16
ROPE_THETA = 10000.0
SSM_HEADDIM = 64
SSM_STATE = 128
SSM_GROUPS = 8
SSM_CONV = 4
SSM_CHUNK = 128
NEG_BIG = -1e30
EPS = 1e-6

V7X_VMEM_BYTES = 64 * 1024 * 1024
V7X_LANES = 128
BF16_SUBLANES = 16
VMEM_LIMIT = (V7X_VMEM_BYTES * 7) // 8
LOG2E = math.log2(math.e)
NA_QROWS = 4
CONV_ROW_BLOCKS = 3
DIFF_KEY_CHUNK = 2048
SSD_CHUNK_UNROLL = 4
MOD_SLOTS = 16
CONV_HALO = BF16_SUBLANES


def _params(sem, vmem=VMEM_LIMIT):
    return pltpu.CompilerParams(dimension_semantics=sem, vmem_limit_bytes=vmem)


def _pick(n, want):
    t = min(n, want)
    while n % t:
        t -= 1
    return t


def _silu(a):
    return a * jax.nn.sigmoid(a)


def _modulate(x, g, shift, scale):
    ms = jnp.mean(x * x, axis=-1, keepdims=True)
    return (x * lax.rsqrt(ms + EPS)) * (g * (1.0 + scale)) + shift


def _mod_spec(kind, slot0, rows_per_slot, tm, d, col=False):
    if col:
        return pl.BlockSpec((None, None, 1, d), lambda i, j: (kind, slot0 + (i * tm) // rows_per_slot, 0, j))
    return pl.BlockSpec((None, None, 1, d), lambda i, j: (kind, slot0 + (i * tm) // rows_per_slot, 0, 0))


def _mod_kernel(c_ref, w_ref, b_ref, o_ref):
    a = _silu(c_ref[...])
    o_ref[...] = jnp.dot(a, w_ref[...], preferred_element_type=F32, precision=lax.Precision.HIGHEST) + b_ref[...]


def _modulation(cvec, mod_w, mod_b):
    depth, d, n = mod_w.shape
    tn = _pick(n, 1024)
    return pl.pallas_call(
        _mod_kernel,
        grid=(depth, n // tn),
        in_specs=[
            pl.BlockSpec((MOD_SLOTS, d), lambda l, j: (0, 0)),
            pl.BlockSpec((None, d, tn), lambda l, j: (l, 0, j)),
            pl.BlockSpec((None, 1, tn), lambda l, j: (l, 0, j)),
        ],
        out_specs=pl.BlockSpec((None, MOD_SLOTS, tn), lambda l, j: (l, 0, j)),
        out_shape=jax.ShapeDtypeStruct((depth, MOD_SLOTS, n), F32),
        compiler_params=_params(("parallel", "parallel")),
        name="modulation",
    )(cvec, mod_w, mod_b.reshape(depth, 1, n))


def _ffn_kernel(*refs, final):
    if final:
        x_ref, sh_ref, sc_ref, gt_ref, g_ref, w1_ref, w3_ref, w2_ref, fg_ref, o_ref, h_ref = refs
    else:
        x_ref, sh_ref, sc_ref, gt_ref, g_ref, w1_ref, w3_ref, w2_ref, o_ref, h_ref = refs
    j = pl.program_id(1)

    @pl.when(j == 0)
    def _():
        h_ref[...] = _modulate(x_ref[...], g_ref[...], sh_ref[...], sc_ref[...]).astype(BF16)
        o_ref[...] = jnp.zeros_like(o_ref)

    h = h_ref[...]
    a = jnp.dot(h, w1_ref[...], preferred_element_type=F32)
    b = jnp.dot(h, w3_ref[...], preferred_element_type=F32)
    o_ref[...] += jnp.dot((_silu(a) * b).astype(BF16), w2_ref[...], preferred_element_type=F32)

    @pl.when(j == pl.num_programs(1) - 1)
    def _():
        y = x_ref[...] + (0.5 * gt_ref[...]) * o_ref[...]
        if final:
            ms = jnp.mean(y * y, axis=-1, keepdims=True)
            y = (y * lax.rsqrt(ms + EPS)) * fg_ref[...]
        o_ref[...] = y


def _ffn(x, mod_l, kinds, slot0, rows_per_slot, norm_g, w1, w3, w2, final_g=None):
    m, d = x.shape
    f = w1.shape[1]
    tm = _pick(rows_per_slot, 512)
    tn = _pick(f, 512)
    final = final_g is not None
    k_sh, k_sc, k_gt = kinds
    in_specs = [
        pl.BlockSpec((tm, d), lambda i, j: (i, 0)),
        _mod_spec(k_sh, slot0, rows_per_slot, tm, d),
        _mod_spec(k_sc, slot0, rows_per_slot, tm, d),
        _mod_spec(k_gt, slot0, rows_per_slot, tm, d),
        pl.BlockSpec((1, d), lambda i, j: (0, 0)),
        pl.BlockSpec((d, tn), lambda i, j: (0, j)),
        pl.BlockSpec((d, tn), lambda i, j: (0, j)),
        pl.BlockSpec((tn, d), lambda i, j: (j, 0)),
    ]
    args = [x, mod_l, mod_l, mod_l, norm_g.reshape(1, d), w1, w3, w2]
    if final:
        in_specs.append(pl.BlockSpec((1, d), lambda i, j: (0, 0)))
        args.append(final_g.reshape(1, d))
    return pl.pallas_call(
        functools.partial(_ffn_kernel, final=final),
        grid=(m // tm, f // tn),
        in_specs=in_specs,
        out_specs=pl.BlockSpec((tm, d), lambda i, j: (i, 0)),
        out_shape=jax.ShapeDtypeStruct((m, d), F32),
        scratch_shapes=[pltpu.VMEM((tm, d), BF16)],
        compiler_params=_params(("parallel", "arbitrary")),
        name="ffn",
    )(*args)


def _rope_tile(res, cos, sin_a, sin_b):
    outs = []
    for k in range(res.shape[1] // HEAD_DIM):
        r = res[:, k * HEAD_DIM:(k + 1) * HEAD_DIM]
        up = pltpu.roll(r, HEAD_DIM - HEAD_DIM // 4, 1)
        dn = pltpu.roll(r, HEAD_DIM // 4, 1)
        outs.append(r * cos + up * sin_a + dn * sin_b)
    return jnp.concatenate(outs, axis=1)


def _proj_kernel(*refs, rope_tiles):
    if rope_tiles:
        x_ref, sh_ref, sc_ref, g_ref, w_ref, cos_ref, sa_ref, sb_ref, o_ref, h_ref = refs
    else:
        x_ref, sh_ref, sc_ref, g_ref, w_ref, o_ref, h_ref = refs
    j = pl.program_id(1)

    @pl.when(j == 0)
    def _():
        h_ref[...] = _modulate(x_ref[...], g_ref[...], sh_ref[...], sc_ref[...]).astype(BF16)

    res = jnp.dot(h_ref[...], w_ref[...], preferred_element_type=F32)
    if rope_tiles:
        is_rope = functools.reduce(jnp.logical_or, [j == t for t in rope_tiles])

        @pl.when(is_rope)
        def _():
            o_ref[...] = _rope_tile(res, cos_ref[...], sa_ref[...], sb_ref[...]).astype(o_ref.dtype)

        @pl.when(jnp.logical_not(is_rope))
        def _():
            o_ref[...] = res.astype(o_ref.dtype)
    else:
        o_ref[...] = res.astype(o_ref.dtype)


def _proj(x, mod_l, kinds, slot0, rows_per_slot, norm_g, w, out_dtype, tn_want, rope=None, cols=None):
    m, d = x.shape
    c0, c1 = cols if cols is not None else (0, w.shape[1])
    n = c1 - c0
    tm = _pick(rows_per_slot, 1024)
    tn = _pick(math.gcd(n, c0) if c0 else n, tn_want)
    cb0 = c0 // tn
    k_sh, k_sc = kinds
    in_specs = [
        pl.BlockSpec((tm, d), lambda i, j: (i, 0)),
        _mod_spec(k_sh, slot0, rows_per_slot, tm, d),
        _mod_spec(k_sc, slot0, rows_per_slot, tm, d),
        pl.BlockSpec((1, d), lambda i, j: (0, 0)),
        pl.BlockSpec((d, tn), lambda i, j: (0, cb0 + j)),
    ]
    args = [x, mod_l, mod_l, norm_g.reshape(1, d), w]
    rope_tiles = ()
    if rope is not None:
        tables, rope_cols = rope
        assert rope_cols[0] % tn == 0 and rope_cols[1] % tn == 0
        rope_tiles = tuple(range(rope_cols[0] // tn, rope_cols[1] // tn))
        tps = rows_per_slot // tm
        for t in tables:
            in_specs.append(pl.BlockSpec((tm, HEAD_DIM), lambda i, j: (i % tps, 0)))
            args.append(t)
    return pl.pallas_call(
        functools.partial(_proj_kernel, rope_tiles=rope_tiles),
        grid=(m // tm, n // tn),
        in_specs=in_specs,
        out_specs=pl.BlockSpec((tm, tn), lambda i, j: (i, j)),
        out_shape=jax.ShapeDtypeStruct((m, n), out_dtype),
        scratch_shapes=[pltpu.VMEM((tm, d), BF16)],
        compiler_params=_params(("parallel", "arbitrary")),
        name="proj",
    )(*args)


def _proj_conv_kernel(xp_ref, x_ref, xn_ref, sh_ref, sc_ref, g_ref, w_ref, cw_ref, cb_ref, o_ref, h_ref,
                      *, tm, n_plain, tiles_per_seq):
    i = pl.program_id(0)
    j = pl.program_id(1)
    n_all = tm + 2 * CONV_HALO

    @pl.when(j == 0)
    def _():
        g, sh, sc = g_ref[...], sh_ref[...], sc_ref[...]
        t = i % tiles_per_seq
        keep_prev = (t != 0).astype(F32)
        keep_next = (t != tiles_per_seq - 1).astype(F32)
        h_ref[0:CONV_HALO, :] = (_modulate(xp_ref[...], g, sh, sc) * keep_prev).astype(BF16)
        h_ref[CONV_HALO:CONV_HALO + tm, :] = _modulate(x_ref[...], g, sh, sc).astype(BF16)
        h_ref[CONV_HALO + tm:n_all, :] = (_modulate(xn_ref[...], g, sh, sc) * keep_next).astype(BF16)

    @pl.when(j < n_plain)
    def _():
        o_ref[...] = jnp.dot(h_ref[CONV_HALO:CONV_HALO + tm, :], w_ref[...],
                             preferred_element_type=F32).astype(o_ref.dtype)

    @pl.when(j >= n_plain)
    def _():
        w = w_ref[...]
        step = n_all // CONV_ROW_BLOCKS if n_all % (CONV_ROW_BLOCKS * BF16_SUBLANES) == 0 else n_all
        u = jnp.concatenate([jnp.dot(h_ref[r0:r0 + step, :], w, preferred_element_type=F32)
                             for r0 in range(0, n_all, step)], axis=0)
        left = (SSM_CONV - 1) // 2
        acc = cb_ref[...] + cw_ref[left:left + 1, :] * u
        for tap in range(SSM_CONV):
            if tap == left:
                continue
            acc = acc + cw_ref[tap:tap + 1, :] * pltpu.roll(u, (left - tap) % n_all, 0)
        y = acc[CONV_HALO:CONV_HALO + tm, :]
        o_ref[...] = _silu(y).astype(o_ref.dtype)


def _proj_conv(x, mod_l, kinds, slot0, rows_per_slot, rows_per_seq, norm_g, w, n_plain_cols, conv_w, conv_b):
    m, d = x.shape
    n = n_plain_cols + conv_w.shape[1]
    tm = _pick(rows_per_seq, 1024)
    tn = _pick(math.gcd(n_plain_cols, n - n_plain_cols), 512)
    n_plain = n_plain_cols // tn
    tps = rows_per_seq // tm
    k_sh, k_sc = kinds
    hb = tm // CONV_HALO
    last_hb = m // CONV_HALO - 1
    in_specs = [
        pl.BlockSpec((CONV_HALO, d), lambda i, j: (jnp.maximum(i * hb - 1, 0), 0)),
        pl.BlockSpec((tm, d), lambda i, j: (i, 0)),
        pl.BlockSpec((CONV_HALO, d), lambda i, j: (jnp.minimum((i + 1) * hb, last_hb), 0)),
        _mod_spec(k_sh, slot0, rows_per_slot, tm, d),
        _mod_spec(k_sc, slot0, rows_per_slot, tm, d),
        pl.BlockSpec((1, d), lambda i, j: (0, 0)),
        pl.BlockSpec((d, tn), lambda i, j: (0, j)),
        pl.BlockSpec((SSM_CONV, tn), lambda i, j: (0, jnp.maximum(j - n_plain, 0))),
        pl.BlockSpec((1, tn), lambda i, j: (0, jnp.maximum(j - n_plain, 0))),
    ]
    return pl.pallas_call(
        functools.partial(_proj_conv_kernel, tm=tm, n_plain=n_plain, tiles_per_seq=tps),
        grid=(m // tm, n // tn),
        in_specs=in_specs,
        out_specs=pl.BlockSpec((tm, tn), lambda i, j: (i, j)),
        out_shape=jax.ShapeDtypeStruct((m, n), BF16),
        scratch_shapes=[pltpu.VMEM((tm + 2 * CONV_HALO, d), BF16)],
        compiler_params=_params(("parallel", "arbitrary")),
        name="proj_conv",
    )(x, x, x, mod_l, mod_l, norm_g.reshape(1, d), w, conv_w, conv_b.reshape(1, -1))


def _outproj_kernel(*refs, n_a):
    x_ref, gt_ref = refs[0], refs[1]
    a_refs = refs[2:2 + n_a]
    w_refs = refs[2 + n_a:2 + 2 * n_a]
    o_ref = refs[2 + 2 * n_a]
    acc = jnp.dot(a_refs[0][...], w_refs[0][...], preferred_element_type=F32)
    for a_ref, w_ref in zip(a_refs[1:], w_refs[1:]):
        acc = acc + jnp.dot(a_ref[...], w_ref[...], preferred_element_type=F32)
    o_ref[...] = x_ref[...] + gt_ref[...] * acc


def _outproj(x, mod_l, k_gt, slot0, rows_per_slot, a_list, w):
    m, d = x.shape
    tm = _pick(rows_per_slot, 1024)
    tn = _pick(d, 512)
    in_specs = [pl.BlockSpec((tm, tn), lambda i, j: (i, j)), _mod_spec(k_gt, slot0, rows_per_slot, tm, tn, col=True)]
    args = [x, mod_l]
    k0 = 0
    w_specs = []
    for a in a_list:
        ka = a.shape[1]
        assert k0 % ka == 0
        in_specs.append(pl.BlockSpec((tm, ka), lambda i, j: (i, 0)))
        w_specs.append(pl.BlockSpec((ka, tn), lambda i, j, kb=k0 // ka: (kb, j)))
        k0 += ka
    assert k0 == w.shape[0]
    return pl.pallas_call(
        functools.partial(_outproj_kernel, n_a=len(a_list)),
        grid=(m // tm, d // tn),
        in_specs=in_specs + w_specs,
        out_specs=pl.BlockSpec((tm, tn), lambda i, j: (i, j)),
        out_shape=jax.ShapeDtypeStruct((m, d), F32),
        compiler_params=_params(("parallel", "parallel")),
        name="outproj",
    )(*args, *a_list, *([w] * len(a_list)))


def _na_plan(rows):
    kr = min(NA_WIN_ROWS, rows)
    qr = NA_QROWS if rows % NA_QROWS == 0 else 1
    slab = min(rows, qr + kr)
    starts, ids, layouts = [], [], []
    for b in range(rows // qr):
        ks = int(np.clip(qr * b - kr // 2, 0, rows - slab))
        lay = []
        for r in range(qr * b, qr * b + qr):
            rs = int(np.clip(r - kr // 2, 0, rows - kr))
            assert ks <= rs and rs + kr <= ks + slab
            lay.append((rs - ks, r - ks))
        lay = tuple(lay)
        if lay not in layouts:
            layouts.append(lay)
        starts.append(ks)
        ids.append(layouts.index(lay))
    return kr, qr, slab, np.array([starts, ids], np.int32), tuple(layouts)


def _na_bias_kernel(rpb_ref, o_ref, *, kr, slab, layouts):
    h = pl.program_id(0)
    n_ro, n_co = 2 * NA_WIN_ROWS - 1, 2 * NA_WIN_COLS - 1
    qc = lax.broadcasted_iota(jnp.int32, (GRID_W, GRID_W), 0)
    kc = lax.broadcasted_iota(jnp.int32, (GRID_W, GRID_W), 1)
    dcol = kc - qc + (NA_WIN_COLS - 1)
    cs = jnp.clip(qc - NA_WIN_COLS // 2, 0, GRID_W - NA_WIN_COLS)
    valid = jnp.logical_and(kc >= cs, kc < cs + NA_WIN_COLS)
    tabs = []
    for ro in range(n_ro):
        acc = jnp.zeros((GRID_W, GRID_W), F32)
        for dd in range(n_co):
            acc = jnp.where(dcol == dd, rpb_ref[(h * n_ro + ro) * n_co + dd], acc)
        tabs.append(jnp.where(valid, acc * LOG2E, NEG_BIG))
    masked = jnp.full((GRID_W, GRID_W), NEG_BIG, F32)
    for ti, lay in enumerate(layouts):
        for qi, (win0, qrow) in enumerate(lay):
            for ki in range(slab):
                ro = ki - qrow + (NA_WIN_ROWS - 1)
                tile = tabs[ro] if win0 <= ki < win0 + kr else masked
                o_ref[ti, qi * GRID_W:(qi + 1) * GRID_W, ki * GRID_W:(ki + 1) * GRID_W] = tile


def _na_bias(rpb, plan):
    kr, qr, slab, _, layouts = plan
    nh = rpb.shape[0]
    shape = (len(layouts), qr * GRID_W, slab * GRID_W)
    return pl.pallas_call(
        functools.partial(_na_bias_kernel, kr=kr, slab=slab, layouts=layouts),
        grid=(nh,),
        in_specs=[pl.BlockSpec(memory_space=pltpu.SMEM)],
        out_specs=pl.BlockSpec((None,) + shape, lambda h: (h, 0, 0, 0)),
        out_shape=jax.ShapeDtypeStruct((nh,) + shape, F32),
        compiler_params=_params(("parallel",)),
        name="na_bias",
    )(rpb.reshape(-1))


def _dot_t(a, b):
    return lax.dot_general(a, b, (((1,), (1,)), ((), ())), preferred_element_type=F32)


def _na_kernel(plan_ref, q_ref, k_ref, v_ref, kc_ref, vc_ref, bias_ref, o_ref, *, n_blocks, nq, nk):
    c = HEAD_DIM ** -0.5 * LOG2E
    kc = kc_ref[...]
    vc = vc_ref[...]

    def block(blk, carry):
        q0 = pl.multiple_of(blk * nq, nq)
        k0 = pl.multiple_of(plan_ref[0, blk] * GRID_W, GRID_W)
        q = q_ref[pl.ds(q0, nq), :]
        s = _dot_t(q, k_ref[pl.ds(k0, nk), :]) * c + bias_ref[plan_ref[1, blk]]
        sc = _dot_t(q, kc) * c
        m = jnp.maximum(jnp.max(s, axis=-1, keepdims=True), jnp.max(sc, axis=-1, keepdims=True))
        p = jnp.exp2(s - m)
        pc = jnp.exp2(sc - m)
        inv = 1.0 / (jnp.sum(p, axis=-1, keepdims=True) + jnp.sum(pc, axis=-1, keepdims=True))
        o = (jnp.dot(p.astype(BF16), v_ref[pl.ds(k0, nk), :], preferred_element_type=F32)
             + jnp.dot(pc.astype(BF16), vc, preferred_element_type=F32)) * inv
        o_ref[pl.ds(q0, nq), :] = o.astype(o_ref.dtype)
        return carry

    lax.fori_loop(0, n_blocks, block, 0)


def _na_attention(qkv_x, qkv_c, bias, plan, batch, seq, ctx_len, n_heads):
    _, qr, slab, plan_arr, layouts = plan
    tab = (len(layouts), qr * GRID_W, slab * GRID_W)
    return pl.pallas_call(
        functools.partial(_na_kernel, n_blocks=plan_arr.shape[1], nq=qr * GRID_W, nk=slab * GRID_W),
        grid=(n_heads, batch),
        in_specs=[
            pl.BlockSpec(memory_space=pltpu.SMEM),
            pl.BlockSpec((seq, HEAD_DIM), lambda h, b: (b, h)),
            pl.BlockSpec((seq, HEAD_DIM), lambda h, b: (b, n_heads + h)),
            pl.BlockSpec((seq, HEAD_DIM), lambda h, b: (b, 2 * n_heads + h)),
            pl.BlockSpec((ctx_len, HEAD_DIM), lambda h, b: (b, n_heads + h)),
            pl.BlockSpec((ctx_len, HEAD_DIM), lambda h, b: (b, 2 * n_heads + h)),
            pl.BlockSpec((None,) + tab, lambda h, b: (h, 0, 0, 0)),
        ],
        out_specs=pl.BlockSpec((seq, HEAD_DIM), lambda h, b: (b, h)),
        out_shape=jax.ShapeDtypeStruct((batch * seq, n_heads * HEAD_DIM), BF16),
        compiler_params=_params(("parallel", "parallel")),
        name="na_attn",
    )(jnp.asarray(plan_arr), qkv_x, qkv_x, qkv_x, qkv_c, qkv_c, bias)


def _ctx_attn_kernel(q_ref, k_ref, v_ref, o_ref):
    s = _dot_t(q_ref[...], k_ref[...]) * HEAD_DIM ** -0.5
    p = jnp.exp(s - jnp.max(s, axis=-1, keepdims=True))
    p = p * (1.0 / jnp.sum(p, axis=-1, keepdims=True))
    o_ref[...] = jnp.dot(p.astype(BF16), v_ref[...], preferred_element_type=F32).astype(o_ref.dtype)


def _ctx_attention(qkv_c, batch, ctx_len, n_heads):
    return pl.pallas_call(
        _ctx_attn_kernel,
        grid=(batch, n_heads),
        in_specs=[
            pl.BlockSpec((ctx_len, HEAD_DIM), lambda b, h: (b, h)),
            pl.BlockSpec((ctx_len, HEAD_DIM), lambda b, h: (b, n_heads + h)),
            pl.BlockSpec((ctx_len, HEAD_DIM), lambda b, h: (b, 2 * n_heads + h)),
        ],
        out_specs=pl.BlockSpec((ctx_len, HEAD_DIM), lambda b, h: (b, h)),
        out_shape=jax.ShapeDtypeStruct((batch * ctx_len, n_heads * HEAD_DIM), BF16),
        compiler_params=_params(("parallel", "parallel")),
        name="ctx_attn",
    )(qkv_c, qkv_c, qkv_c)


def _diff_kernel(*refs, n_src, lam_init, src_rows):
    lam_ref, g_ref, q_ref = refs[0], refs[1], refs[2]
    k_refs = refs[3:3 + n_src]
    v_refs = refs[3 + n_src:3 + 2 * n_src]
    o_ref = refs[3 + 2 * n_src]
    acc_ref = refs[4 + 2 * n_src]
    tq = q_ref.shape[0]
    c = HEAD_DIM ** -0.5 * LOG2E
    lf = lam_ref[...]
    lam = (jnp.exp(jnp.sum(lf[0:1] * lf[1:2], axis=-1, keepdims=True))
           - jnp.exp(jnp.sum(lf[2:3] * lf[3:4], axis=-1, keepdims=True)) + lam_init)
    qs = [q_ref[:, mi * HEAD_DIM:(mi + 1) * HEAD_DIM] for mi in range(2)]
    acc_ref[...] = jnp.zeros_like(acc_ref)

    def chunk(carry, k_ref, v_ref, r0, n):
        rows = pl.ds(r0, n)
        v = v_ref[rows, :]
        out = []
        for mi in range(2):
            m, l = carry[2 * mi], carry[2 * mi + 1]
            s = _dot_t(qs[mi], k_ref[rows, mi * HEAD_DIM:(mi + 1) * HEAD_DIM])
            m_new = jnp.maximum(m, jnp.max(s, axis=-1, keepdims=True))
            a = jnp.exp2((m - m_new) * c)
            p = jnp.exp2((s - m_new) * c)
            acc_ref[mi] = acc_ref[mi] * a + jnp.dot(p.astype(BF16), v, preferred_element_type=F32)
            out += [m_new, l * a + jnp.sum(p, axis=-1, keepdims=True)]
        return tuple(out)

    carry = (jnp.full((tq, 1), NEG_BIG, F32), jnp.zeros((tq, 1), F32)) * 2
    for k_ref, v_ref, rows in zip(k_refs, v_refs, src_rows):
        ck = _pick(rows, DIFF_KEY_CHUNK)
        if rows == ck:
            carry = chunk(carry, k_ref, v_ref, 0, ck)
        else:
            carry = lax.fori_loop(
                0, rows // ck,
                lambda i, cr, k_ref=k_ref, v_ref=v_ref, ck=ck: chunk(cr, k_ref, v_ref, pl.multiple_of(i * ck, ck), ck),
                carry, unroll=True)
    _, l1, _, l2 = carry
    o = acc_ref[0] * (1.0 / l1) - acc_ref[1] * (lam / l2)
    ms = jnp.mean(o * o, axis=-1, keepdims=True)
    o_ref[...] = ((o * lax.rsqrt(ms + EPS)) * g_ref[...] * (1.0 - lam_init)).astype(o_ref.dtype)


def _diff_attention(q_arr, q_rows, kv_list, lam_vecs, subln_g, lam_init, batch, n_heads, q_col0, tq_want):
    hw = 2 * HEAD_DIM
    tq = _pick(q_rows, tq_want)
    nq = q_rows // tq
    qb = q_col0 // hw
    in_specs = [
        pl.BlockSpec((4, HEAD_DIM), lambda b, h, i: (0, 0)),
        pl.BlockSpec((1, hw), lambda b, h, i: (0, 0)),
        pl.BlockSpec((tq, hw), lambda b, h, i: (b * nq + i, qb + h)),
    ]
    args = [lam_vecs, subln_g.reshape(1, hw), q_arr]
    for arr, rows in kv_list:
        in_specs.append(pl.BlockSpec((rows, hw), lambda b, h, i: (b, qb + n_heads + h)))
        args.append(arr)
    for arr, rows in kv_list:
        in_specs.append(pl.BlockSpec((rows, hw), lambda b, h, i: (b, qb + 2 * n_heads + h)))
        args.append(arr)
    return pl.pallas_call(
        functools.partial(_diff_kernel, n_src=len(kv_list), lam_init=lam_init,
                          src_rows=tuple(rows for _, rows in kv_list)),
        grid=(batch, n_heads, nq),
        in_specs=in_specs,
        out_specs=pl.BlockSpec((tq, hw), lambda b, h, i: (b * nq + i, h)),
        out_shape=jax.ShapeDtypeStruct((batch * q_rows, n_heads * hw), BF16),
        scratch_shapes=[pltpu.VMEM((2, tq, hw), F32)],
        compiler_params=_params(("parallel", "parallel", "parallel")),
        name="diff_attn",
    )(*args)


def _cumsum_rows(a, reverse):
    n = a.shape[0]
    row = lax.broadcasted_iota(jnp.int32, a.shape, 0)
    d = 1
    while d < n:
        if reverse:
            a = a + jnp.where(row < n - d, pltpu.roll(a, n - d, 0), 0.0)
        else:
            a = a + jnp.where(row >= d, pltpu.roll(a, d, 0), 0.0)
        d *= 2
    return a


def _ssd_kernel(*refs, n_ctx_chunks, n_x_chunks, hpg, n_heads, ctx_out):
    (xs_x, z_x, b_x, c_x, dt_x, xs_c, z_c, b_c, c_c, dt_c, bias_ref, alog_ref, dsum_ref, ng_ref) = refs[:14]
    n_out = 2 if ctx_out else 1
    yx_ref = refs[14]
    yc_ref = refs[15] if ctx_out else None
    yacc_x, yacc_c, sc_x, sc_c, rft_x, rft_c, hf_ref, hb_ref = refs[14 + n_out:]
    t = SSM_CHUNK
    pw = 2 * SSM_HEADDIM
    g = pl.program_id(1)
    nl = 2 * n_heads

    lane = lax.broadcasted_iota(jnp.int32, (t, nl), 1)
    is_fwd = lane < n_heads
    bias = bias_ref[...]
    neg_a2 = -jnp.exp(alog_ref[...]) * LOG2E
    li = lax.broadcasted_iota(jnp.int32, (t, t), 0)
    si = lax.broadcasted_iota(jnp.int32, (t, t), 1)
    add_lo = jnp.where(li >= si, 0.0, NEG_BIG)
    add_up = jnp.where(li <= si, 0.0, NEG_BIG)
    pair_lane = lax.broadcasted_iota(jnp.int32, (t, pw), 1) < SSM_HEADDIM
    lane_shift = (nl - g * hpg) % nl

    def pair_weights(xp):
        zero = jnp.zeros_like(xp)
        return jnp.concatenate([jnp.where(pair_lane, xp, zero), jnp.where(pair_lane, zero, xp)], axis=0)

    def col(scl, idx):
        return jnp.broadcast_to(scl[:, idx:idx + 1], (t, t))

    def fwd_chunk(xs_ref, b_ref, c_ref, dt_ref, yacc, sc_ref, rft_ref, r0, h0):
        rows = pl.ds(r0, t)
        dt = jax.nn.softplus(dt_ref[rows, :] + bias)
        a2 = dt * neg_a2
        sc2 = jnp.where(is_fwd, _cumsum_rows(a2, False), _cumsum_rows(a2, True))
        sc_ref[rows, :] = sc2
        rft_ref[pl.ds(h0, nl), :] = (sc2 - jnp.log(dt) * LOG2E).T
        scl = pltpu.roll(sc2, lane_shift, 1)
        xb = xs_ref[rows, :]
        bb = b_ref[rows, :]
        cb_ = c_ref[rows, :]
        cbm = _dot_t(cb_, bb)
        bt = bb.astype(F32).T
        yoff = jnp.dot(cb_, hf_ref[...].astype(BF16), preferred_element_type=F32)
        for jp in range(hpg // 2):
            ms, bws, es = [], [], []
            for j in (2 * jp, 2 * jp + 1):
                cf = col(scl, j)
                cbk = col(scl, n_heads + j)
                rf = rft_ref[pl.ds(h0 + g * hpg + j, 1), :]
                rb = rft_ref[pl.ds(h0 + n_heads + g * hpg + j, 1), :]
                ms.append((cbm * (jnp.exp2(cf - rf + add_lo) + jnp.exp2(cbk - rb + add_up))).astype(BF16))
                bws.append((bt * jnp.exp2(cf[t - 1:t, :] - rf)).astype(BF16))
                es.append(jnp.exp2(cf))
            cols = slice(jp * pw, (jp + 1) * pw)
            xp = xb[:, cols]
            lhs = jnp.concatenate([jnp.concatenate(ms, axis=1), jnp.concatenate(bws, axis=1)], axis=0)
            res = jnp.dot(lhs, pair_weights(xp), preferred_element_type=F32)
            e_pair = jnp.where(pair_lane, es[0], es[1])
            yacc[rows, cols] = res[:t] + xp.astype(F32) * dsum_ref[:, cols] + yoff[:, cols] * e_pair
            hf_ref[:, cols] = hf_ref[:, cols] * e_pair[t - 1:t, :] + res[t:]

    def bwd_chunk(xs_ref, z_ref, b_ref, c_ref, yacc, sc_ref, rft_ref, y_ref, r0, h0):
        rows = pl.ds(r0, t)
        scl = pltpu.roll(sc_ref[rows, :], lane_shift, 1)
        xb = xs_ref[rows, :]
        bt = b_ref[rows, :].astype(F32).T
        yoff = jnp.dot(c_ref[rows, :], hb_ref[...].astype(BF16), preferred_element_type=F32)
        ys = []
        for jp in range(hpg // 2):
            bws, es = [], []
            for j in (2 * jp, 2 * jp + 1):
                cbk = col(scl, n_heads + j)
                rb = rft_ref[pl.ds(h0 + n_heads + g * hpg + j, 1), :]
                bws.append((bt * jnp.exp2(cbk[0:1, :] - rb)).astype(BF16))
                es.append(jnp.exp2(cbk))
            cols = slice(jp * pw, (jp + 1) * pw)
            grow = jnp.dot(jnp.concatenate(bws, axis=1), pair_weights(xb[:, cols]), preferred_element_type=F32)
            e_pair = jnp.where(pair_lane, es[0], es[1])
            ys.append(yacc[rows, cols] + yoff[:, cols] * e_pair)
            hb_ref[:, cols] = hb_ref[:, cols] * e_pair[0:1, :] + grow
        if y_ref is not None:
            yg = jnp.concatenate(ys, axis=1) * _silu(z_ref[rows, :].astype(F32))
            ms = jnp.mean(yg * yg, axis=-1, keepdims=True)
            y_ref[rows, :] = ((yg * lax.rsqrt(ms + EPS)) * ng_ref[...]).astype(y_ref.dtype)

    hf_ref[...] = jnp.zeros_like(hf_ref)
    hb_ref[...] = jnp.zeros_like(hb_ref)
    for k in range(n_ctx_chunks):
        fwd_chunk(xs_c, b_c, c_c, dt_c, yacc_c, sc_c, rft_c, k * t, k * nl)

    def fx(k, carry):
        fwd_chunk(xs_x, b_x, c_x, dt_x, yacc_x, sc_x, rft_x, pl.multiple_of(k * t, t), pl.multiple_of(k * nl, nl))
        return carry

    lax.fori_loop(0, n_x_chunks, fx, 0, unroll=SSD_CHUNK_UNROLL)
    for k in reversed(range(n_ctx_chunks)):
        bwd_chunk(xs_c, z_c, b_c, c_c, yacc_c, sc_c, rft_c, yc_ref, k * t, k * nl)

    def bx(k, carry):
        kk = n_x_chunks - 1 - k
        bwd_chunk(xs_x, z_x, b_x, c_x, yacc_x, sc_x, rft_x, yx_ref, pl.multiple_of(kk * t, t),
                  pl.multiple_of(kk * nl, nl))
        return carry

    lax.fori_loop(0, n_x_chunks, bx, 0, unroll=SSD_CHUNK_UNROLL)


def _ssd(zx, zc, dtx, dtc, a_log, dt_bias, d_skip, norm_g, batch, seq, ctx_len, ctx_out):
    inner = norm_g.shape[0]
    n_heads = inner // SSM_HEADDIM
    hpg = n_heads // SSM_GROUPS
    gw = hpg * SSM_HEADDIM
    nl = 2 * n_heads
    assert hpg % 2 == 0 and seq % SSM_CHUNK == 0 and ctx_len % SSM_CHUNK == 0
    assert SSM_CHUNK == 2 * SSM_HEADDIM == SSM_STATE == V7X_LANES and nl == V7X_LANES
    zb = inner // gw
    b0 = (2 * inner) // SSM_STATE
    c0 = b0 + SSM_GROUPS

    def specs(rows):
        return [
            pl.BlockSpec((rows, gw), lambda b, g: (b, zb + g)),
            pl.BlockSpec((rows, gw), lambda b, g: (b, g)),
            pl.BlockSpec((rows, SSM_STATE), lambda b, g: (b, b0 + g)),
            pl.BlockSpec((rows, SSM_STATE), lambda b, g: (b, c0 + g)),
            pl.BlockSpec((rows, nl), lambda b, g: (b, 0)),
        ]

    dsum = jnp.repeat(d_skip[0] + d_skip[1], SSM_HEADDIM).reshape(1, inner).astype(F32)
    in_specs = specs(seq) + specs(ctx_len) + [
        pl.BlockSpec((1, nl), lambda b, g: (0, 0)),
        pl.BlockSpec((1, nl), lambda b, g: (0, 0)),
        pl.BlockSpec((1, gw), lambda b, g: (0, g)),
        pl.BlockSpec((1, gw), lambda b, g: (0, g)),
    ]
    out_specs = [pl.BlockSpec((seq, gw), lambda b, g: (b, g))]
    out_shape = [jax.ShapeDtypeStruct((batch * seq, inner), BF16)]
    if ctx_out:
        out_specs.append(pl.BlockSpec((ctx_len, gw), lambda b, g: (b, g)))
        out_shape.append(jax.ShapeDtypeStruct((batch * ctx_len, inner), BF16))
    ncx, ncc = seq // SSM_CHUNK, ctx_len // SSM_CHUNK
    scratch = [
        pltpu.VMEM((seq, gw), F32), pltpu.VMEM((ctx_len, gw), F32),
        pltpu.VMEM((seq, nl), F32), pltpu.VMEM((ctx_len, nl), F32),
        pltpu.VMEM((ncx * nl, SSM_CHUNK), F32), pltpu.VMEM((ncc * nl, SSM_CHUNK), F32),
        pltpu.VMEM((SSM_STATE, gw), F32), pltpu.VMEM((SSM_STATE, gw), F32),
    ]
    outs = pl.pallas_call(
        functools.partial(_ssd_kernel, n_ctx_chunks=ctx_len // SSM_CHUNK, n_x_chunks=seq // SSM_CHUNK,
                          hpg=hpg, n_heads=n_heads, ctx_out=ctx_out),
        grid=(batch, SSM_GROUPS),
        in_specs=in_specs,
        out_specs=out_specs,
        out_shape=out_shape,
        scratch_shapes=scratch,
        compiler_params=_params(("parallel", "parallel")),
        name="ssd",
    )(zx, zx, zx, zx, dtx, zc, zc, zc, zc, dtc,
      dt_bias.reshape(1, nl), a_log.reshape(1, nl), dsum, norm_g.reshape(1, inner))
    return (outs[0], outs[1]) if ctx_out else (outs[0], None)


def _rope_tables(seq):
    quarter = HEAD_DIM // 4
    inv = 1.0 / (ROPE_THETA ** (jnp.arange(quarter, dtype=F32) / quarter))
    t = jnp.arange(seq)
    row = (t // GRID_W).astype(F32)[:, None] * inv
    col = (t % GRID_W).astype(F32)[:, None] * inv
    ang = jnp.concatenate([row, row, col, col], axis=-1)
    cos, sin = jnp.cos(ang), jnp.sin(ang)
    first = (np.arange(HEAD_DIM) % (2 * quarter)) < quarter
    return cos, jnp.where(first, -sin, 0.0), jnp.where(first, 0.0, sin)


def kernel(x, c, ctx, c_ctx, mod_w, mod_b, norm_g, ffn_w1, ffn_w3, ffn_w2, attn_w_in, attn_w_out, na_rpb,
           diff_lambda, diff_subln_g, ssm_w_in, ssm_conv_w, ssm_conv_b, ssm_a_log, ssm_dt_bias, ssm_d,
           ssm_norm_g, ssm_w_out, final_norm_g):
    batch, seq, d = x.shape
    ctx_len = ctx.shape[1]
    depth = mod_w.shape[0]
    assert batch + 1 <= MOD_SLOTS
    na_w = d // 2
    na_heads = na_w // HEAD_DIM
    diff_heads = na_w // (2 * HEAD_DIM)
    inner = ssm_norm_g.shape[1]
    n_ssm_heads = inner // SSM_HEADDIM

    cvec = jnp.zeros((MOD_SLOTS, d), F32).at[:batch].set(c).at[batch].set(c_ctx)
    mod = _modulation(cvec, mod_w, mod_b)
    mod = mod.reshape(depth, MOD_SLOTS, N_MOD, d).transpose(0, 2, 1, 3)[:, :, :, None, :]

    xs = x.reshape(batch * seq, d)
    hs = ctx.reshape(batch * ctx_len, d)
    rope = _rope_tables(seq)
    na_plan = _na_plan(seq // GRID_W)
    xslot = (0, seq)
    cslot = (batch, batch * ctx_len)

    for layer in range(depth):
        ctx_out = layer < depth - 1
        last = layer == depth - 1
        ml = mod[layer]
        ng = norm_g[layer]
        w1a, w3a, w2a = (w[layer, 0].astype(BF16) for w in (ffn_w1, ffn_w3, ffn_w2))
        w1b, w3b, w2b = (w[layer, 1].astype(BF16) for w in (ffn_w1, ffn_w3, ffn_w2))

        xs = _ffn(xs, ml, (0, 1, 2), *xslot, ng[0], w1a, w3a, w2a)
        hs = _ffn(hs, ml, (0, 1, 2), *cslot, ng[0], w1a, w3a, w2a)

        i = layer // 2
        if layer % 2 == 0:
            lam_init = 0.8 - 0.6 * math.exp(-0.3 * layer)
            w_in = attn_w_in[i].astype(BF16)
            w_out = attn_w_out[i].astype(BF16)
            dq0 = 3 * na_w
            qkv_x = _proj(xs, ml, (3, 4), *xslot, ng[1], w_in, BF16, 1024, rope=(rope, (dq0, dq0 + 2 * na_w)))
            qkv_c = _proj(hs, ml, (3, 4), *cslot, ng[1], w_in, BF16, 1024)
            bias = _na_bias(na_rpb[i], na_plan)
            na_x = _na_attention(qkv_x, qkv_c, bias, na_plan, batch, seq, ctx_len, na_heads)
            df_x = _diff_attention(qkv_x, seq, [(qkv_x, seq), (qkv_c, ctx_len)], diff_lambda[i], diff_subln_g[i],
                                   lam_init, batch, diff_heads, dq0, 512)
            xs = _outproj(xs, ml, 5, *xslot, [na_x, df_x], w_out)
            if ctx_out:
                na_c = _ctx_attention(qkv_c, batch, ctx_len, na_heads)
                df_c = _diff_attention(qkv_c, ctx_len, [(qkv_c, ctx_len)], diff_lambda[i], diff_subln_g[i],
                                       lam_init, batch, diff_heads, dq0, 256)
                hs = _outproj(hs, ml, 5, *cslot, [na_c, df_c], w_out)
        else:
            n_zx = 2 * inner + 2 * SSM_GROUPS * SSM_STATE
            w_in = ssm_w_in[i].astype(BF16)
            dt_cols = (n_zx, w_in.shape[1])
            cw, cb = ssm_conv_w[i], ssm_conv_b[i]
            zx = _proj_conv(xs, ml, (3, 4), *xslot, seq, ng[1], w_in, inner, cw, cb)
            zc = _proj_conv(hs, ml, (3, 4), *cslot, ctx_len, ng[1], w_in, inner, cw, cb)
            dtx = _proj(xs, ml, (3, 4), *xslot, ng[1], w_in, F32, 2 * n_ssm_heads, cols=dt_cols)
            dtc = _proj(hs, ml, (3, 4), *cslot, ng[1], w_in, F32, 2 * n_ssm_heads, cols=dt_cols)
            yx, yc = _ssd(zx, zc, dtx, dtc, ssm_a_log[i], ssm_dt_bias[i], ssm_d[i], ssm_norm_g[i],
                          batch, seq, ctx_len, ctx_out)
            w_out = ssm_w_out[i].astype(BF16)
            xs = _outproj(xs, ml, 5, *xslot, [yx], w_out)
            if ctx_out:
                hs = _outproj(hs, ml, 5, *cslot, [yc], w_out)

        xs = _ffn(xs, ml, (6, 7, 8), *xslot, ng[2], w1b, w3b, w2b, final_g=final_norm_g if last else None)
        if ctx_out:
            hs = _ffn(hs, ml, (6, 7, 8), *cslot, ng[2], w1b, w3b, w2b)
    return xs.reshape(batch, seq, d)
```

```python
import functools
import math

import jax
import jax.numpy as jnp
import numpy as np
from jax import lax
from jax.experimental import pallas as pl
from jax.experimental.pallas import tpu as pltpu

F32 = jnp.float32
BF16 = jnp.bfloat16

HEAD_DIM = 128
GRID_W = 64
N_MOD = 9
NA_WIN_ROWS = 8
NA_WIN_COLS = 16
ROPE_THETA = 10000.0
SSM_HEADDIM = 64
SSM_STATE = 128
SSM_GROUPS = 8
SSM_CONV = 4
SSM_CHUNK = 128
NEG_BIG = -1e30
EPS = 1e-6

V7X_VMEM_BYTES = 64 * 1024 * 1024
V7X_LANES = 128
BF16_SUBLANES = 16
VMEM_LIMIT = (V7X_VMEM_BYTES * 7) // 8
LOG2E = math.log2(math.e)
NA_QROWS = 4
CONV_ROW_BLOCKS = 3
DIFF_KEY_CHUNK = 2048
SSD_CHUNK_UNROLL = 4
MOD_SLOTS = 16
CONV_HALO = BF16_SUBLANES


def _params(sem, vmem=VMEM_LIMIT):
    return pltpu.CompilerParams(dimension_semantics=sem, vmem_limit_bytes=vmem)


def _pick(n, want):
    t = min(n, want)
    while n % t:
        t -= 1
    return t


def _silu(a):
    return a * jax.nn.sigmoid(a)


def _modulate(x, g, shift, scale):
    ms = jnp.mean(x * x, axis=-1, keepdims=True)
    return (x * lax.rsqrt(ms + EPS)) * (g * (1.0 + scale)) + shift


def _mod_spec(kind, slot0, rows_per_slot, tm, d, col=False):
    if col:
        return pl.BlockSpec((None, None, 1, d), lambda i, j: (kind, slot0 + (i * tm) // rows_per_slot, 0, j))
    return pl.BlockSpec((None, None, 1, d), lambda i, j: (kind, slot0 + (i * tm) // rows_per_slot, 0, 0))


def _mod_kernel(c_ref, w_ref, b_ref, o_ref):
    a = _silu(c_ref[...])
    o_ref[...] = jnp.dot(a, w_ref[...], preferred_element_type=F32, precision=lax.Precision.HIGHEST) + b_ref[...]


def _modulation(cvec, mod_w, mod_b):
    depth, d, n = mod_w.shape
    tn = _pick(n, 1024)
    return pl.pallas_call(
        _mod_kernel,
        grid=(depth, n // tn),
        in_specs=[
            pl.BlockSpec((MOD_SLOTS, d), lambda l, j: (0, 0)),
            pl.BlockSpec((None, d, tn), lambda l, j: (l, 0, j)),
            pl.BlockSpec((None, 1, tn), lambda l, j: (l, 0, j)),
        ],
        out_specs=pl.BlockSpec((None, MOD_SLOTS, tn), lambda l, j: (l, 0, j)),
        out_shape=jax.ShapeDtypeStruct((depth, MOD_SLOTS, n), F32),
        compiler_params=_params(("parallel", "parallel")),
        name="modulation",
    )(cvec, mod_w, mod_b.reshape(depth, 1, n))


def _ffn_kernel(*refs, final, n_chunks):
    if final:
        x_ref, sh_ref, sc_ref, gt_ref, g_ref, w1_ref, w3_ref, w2_ref, fg_ref, o_ref, h_ref, ga_ref, gb_ref = refs
    else:
        x_ref, sh_ref, sc_ref, gt_ref, g_ref, w1_ref, w3_ref, w2_ref, o_ref, h_ref, ga_ref, gb_ref = refs
    j = pl.program_id(1)
    bufs = (ga_ref, gb_ref)

    def up(dst):
        h = h_ref[...]
        a = jnp.dot(h, w1_ref[...], preferred_element_type=F32)
        b = jnp.dot(h, w3_ref[...], preferred_element_type=F32)
        dst[...] = (_silu(a) * b).astype(BF16)

    def down(src):
        o_ref[...] += jnp.dot(src[...], w2_ref[...], preferred_element_type=F32)

    @pl.when(j == 0)
    def _():
        h_ref[...] = _modulate(x_ref[...], g_ref[...], sh_ref[...], sc_ref[...]).astype(BF16)
        o_ref[...] = jnp.zeros_like(o_ref)
        up(ga_ref)

    for parity in (0, 1):
        @pl.when(jnp.logical_and(jnp.logical_and(j >= 1, j < n_chunks), j % 2 == parity))
        def _():
            down(bufs[1 - parity])
            up(bufs[parity])

    @pl.when(j == n_chunks)
    def _():
        down(bufs[(n_chunks - 1) % 2])
        y = x_ref[...] + (0.5 * gt_ref[...]) * o_ref[...]
        if final:
            ms = jnp.mean(y * y, axis=-1, keepdims=True)
            y = (y * lax.rsqrt(ms + EPS)) * fg_ref[...]
        o_ref[...] = y


def _ffn(x, mod_l, kinds, slot0, rows_per_slot, norm_g, w1, w3, w2, widx, final_g=None):
    m, d = x.shape
    f = w1.shape[-1]
    tm = _pick(rows_per_slot, 512)
    tn = _pick(f, 512)
    nc = f // tn
    final = final_g is not None
    k_sh, k_sc, k_gt = kinds
    la, wh = widx
    in_specs = [
        pl.BlockSpec((tm, d), lambda i, j: (i, 0)),
        _mod_spec(k_sh, slot0, rows_per_slot, tm, d),
        _mod_spec(k_sc, slot0, rows_per_slot, tm, d),
        _mod_spec(k_gt, slot0, rows_per_slot, tm, d),
        pl.BlockSpec((1, d), lambda i, j: (0, 0)),
        pl.BlockSpec((None, None, d, tn), lambda i, j: (la, wh, 0, jnp.minimum(j, nc - 1))),
        pl.BlockSpec((None, None, d, tn), lambda i, j: (la, wh, 0, jnp.minimum(j, nc - 1))),
        pl.BlockSpec((None, None, tn, d), lambda i, j: (la, wh, jnp.maximum(j - 1, 0), 0)),
    ]
    args = [x, mod_l, mod_l, mod_l, norm_g.reshape(1, d), w1, w3, w2]
    if final:
        in_specs.append(pl.BlockSpec((1, d), lambda i, j: (0, 0)))
        args.append(final_g.reshape(1, d))
    return pl.pallas_call(
        functools.partial(_ffn_kernel, final=final, n_chunks=nc),
        grid=(m // tm, nc + 1),
        in_specs=in_specs,
        out_specs=pl.BlockSpec((tm, d), lambda i, j: (i, 0)),
        out_shape=jax.ShapeDtypeStruct((m, d), F32),
        scratch_shapes=[pltpu.VMEM((tm, d), BF16), pltpu.VMEM((tm, tn), BF16), pltpu.VMEM((tm, tn), BF16)],
        compiler_params=_params(("parallel", "arbitrary")),
        name="ffn",
    )(*args)


def _rope_tile(res, cos, sin_a, sin_b):
    outs = []
    for k in range(res.shape[1] // HEAD_DIM):
        r = res[:, k * HEAD_DIM:(k + 1) * HEAD_DIM]
        up = pltpu.roll(r, HEAD_DIM - HEAD_DIM // 4, 1)
        dn = pltpu.roll(r, HEAD_DIM // 4, 1)
        outs.append(r * cos + up * sin_a + dn * sin_b)
    return jnp.concatenate(outs, axis=1)


def _proj_kernel(*refs, rope_tiles):
    if rope_tiles:
        x_ref, sh_ref, sc_ref, g_ref, w_ref, cos_ref, sa_ref, sb_ref, o_ref, h_ref = refs
    else:
        x_ref, sh_ref, sc_ref, g_ref, w_ref, o_ref, h_ref = refs
    j = pl.program_id(1)

    @pl.when(j == 0)
    def _():
        h_ref[...] = _modulate(x_ref[...], g_ref[...], sh_ref[...], sc_ref[...]).astype(BF16)

    res = jnp.dot(h_ref[...], w_ref[...], preferred_element_type=F32)
    if rope_tiles:
        is_rope = functools.reduce(jnp.logical_or, [j == t for t in rope_tiles])

        @pl.when(is_rope)
        def _():
            o_ref[...] = _rope_tile(res, cos_ref[...], sa_ref[...], sb_ref[...]).astype(o_ref.dtype)

        @pl.when(jnp.logical_not(is_rope))
        def _():
            o_ref[...] = res.astype(o_ref.dtype)
    else:
        o_ref[...] = res.astype(o_ref.dtype)


def _proj(x, mod_l, kinds, slot0, rows_per_slot, norm_g, w, out_dtype, tn_want, rope=None, cols=None):
    m, d = x.shape
    c0, c1 = cols if cols is not None else (0, w.shape[1])
    n = c1 - c0
    tm = _pick(rows_per_slot, 1024)
    tn = _pick(math.gcd(n, c0) if c0 else n, tn_want)
    cb0 = c0 // tn
    k_sh, k_sc = kinds
    in_specs = [
        pl.BlockSpec((tm, d), lambda i, j: (i, 0)),
        _mod_spec(k_sh, slot0, rows_per_slot, tm, d),
        _mod_spec(k_sc, slot0, rows_per_slot, tm, d),
        pl.BlockSpec((1, d), lambda i, j: (0, 0)),
        pl.BlockSpec((d, tn), lambda i, j: (0, cb0 + j)),
    ]
    args = [x, mod_l, mod_l, norm_g.reshape(1, d), w]
    rope_tiles = ()
    if rope is not None:
        tables, rope_cols = rope
        assert rope_cols[0] % tn == 0 and rope_cols[1] % tn == 0
        rope_tiles = tuple(range(rope_cols[0] // tn, rope_cols[1] // tn))
        tps = rows_per_slot // tm
        for t in tables:
            in_specs.append(pl.BlockSpec((tm, HEAD_DIM), lambda i, j: (i % tps, 0)))
            args.append(t)
    return pl.pallas_call(
        functools.partial(_proj_kernel, rope_tiles=rope_tiles),
        grid=(m // tm, n // tn),
        in_specs=in_specs,
        out_specs=pl.BlockSpec((tm, tn), lambda i, j: (i, j)),
        out_shape=jax.ShapeDtypeStruct((m, n), out_dtype),
        scratch_shapes=[pltpu.VMEM((tm, d), BF16)],
        compiler_params=_params(("parallel", "arbitrary")),
        name="proj",
    )(*args)


def _proj_conv_kernel(xp_ref, x_ref, xn_ref, sh_ref, sc_ref, g_ref, w_ref, cw_ref, cb_ref, o_ref, h_ref,
                      *, tm, n_plain, tiles_per_seq):
    i = pl.program_id(0)
    j = pl.program_id(1)
    n_all = tm + 2 * CONV_HALO

    @pl.when(j == 0)
    def _():
        g, sh, sc = g_ref[...], sh_ref[...], sc_ref[...]
        t = i % tiles_per_seq
        keep_prev = (t != 0).astype(F32)
        keep_next = (t != tiles_per_seq - 1).astype(F32)
        h_ref[0:CONV_HALO, :] = (_modulate(xp_ref[...], g, sh, sc) * keep_prev).astype(BF16)
        h_ref[CONV_HALO:CONV_HALO + tm, :] = _modulate(x_ref[...], g, sh, sc).astype(BF16)
        h_ref[CONV_HALO + tm:n_all, :] = (_modulate(xn_ref[...], g, sh, sc) * keep_next).astype(BF16)

    @pl.when(j < n_plain)
    def _():
        o_ref[...] = jnp.dot(h_ref[CONV_HALO:CONV_HALO + tm, :], w_ref[...],
                             preferred_element_type=F32).astype(o_ref.dtype)

    @pl.when(j >= n_plain)
    def _():
        w = w_ref[...]
        step = n_all // CONV_ROW_BLOCKS if n_all % (CONV_ROW_BLOCKS * BF16_SUBLANES) == 0 else n_all
        u = jnp.concatenate([jnp.dot(h_ref[r0:r0 + step, :], w, preferred_element_type=F32)
                             for r0 in range(0, n_all, step)], axis=0)
        left = (SSM_CONV - 1) // 2
        acc = cb_ref[...] + cw_ref[left:left + 1, :] * u
        for tap in range(SSM_CONV):
            if tap == left:
                continue
            acc = acc + cw_ref[tap:tap + 1, :] * pltpu.roll(u, (left - tap) % n_all, 0)
        y = acc[CONV_HALO:CONV_HALO + tm, :]
        o_ref[...] = _silu(y).astype(o_ref.dtype)


def _proj_conv(x, mod_l, kinds, slot0, rows_per_slot, rows_per_seq, norm_g, w, n_plain_cols, conv_w, conv_b):
    m, d = x.shape
    n = n_plain_cols + conv_w.shape[1]
    tm = _pick(rows_per_seq, 1024)
    tn = _pick(math.gcd(n_plain_cols, n - n_plain_cols), 512)
    n_plain = n_plain_cols // tn
    tps = rows_per_seq // tm
    k_sh, k_sc = kinds
    hb = tm // CONV_HALO
    last_hb = m // CONV_HALO - 1
    in_specs = [
        pl.BlockSpec((CONV_HALO, d), lambda i, j: (jnp.maximum(i * hb - 1, 0), 0)),
        pl.BlockSpec((tm, d), lambda i, j: (i, 0)),
        pl.BlockSpec((CONV_HALO, d), lambda i, j: (jnp.minimum((i + 1) * hb, last_hb), 0)),
        _mod_spec(k_sh, slot0, rows_per_slot, tm, d),
        _mod_spec(k_sc, slot0, rows_per_slot, tm, d),
        pl.BlockSpec((1, d), lambda i, j: (0, 0)),
        pl.BlockSpec((d, tn), lambda i, j: (0, j)),
        pl.BlockSpec((SSM_CONV, tn), lambda i, j: (0, jnp.maximum(j - n_plain, 0))),
        pl.BlockSpec((1, tn), lambda i, j: (0, jnp.maximum(j - n_plain, 0))),
    ]
    return pl.pallas_call(
        functools.partial(_proj_conv_kernel, tm=tm, n_plain=n_plain, tiles_per_seq=tps),
        grid=(m // tm, n // tn),
        in_specs=in_specs,
        out_specs=pl.BlockSpec((tm, tn), lambda i, j: (i, j)),
        out_shape=jax.ShapeDtypeStruct((m, n), BF16),
        scratch_shapes=[pltpu.VMEM((tm + 2 * CONV_HALO, d), BF16)],
        compiler_params=_params(("parallel", "arbitrary")),
        name="proj_conv",
    )(x, x, x, mod_l, mod_l, norm_g.reshape(1, d), w, conv_w, conv_b.reshape(1, -1))


def _outproj_kernel(*refs, n_a):
    x_ref, gt_ref = refs[0], refs[1]
    a_refs = refs[2:2 + n_a]
    w_refs = refs[2 + n_a:2 + 2 * n_a]
    o_ref = refs[2 + 2 * n_a]
    acc = jnp.dot(a_refs[0][...], w_refs[0][...], preferred_element_type=F32)
    for a_ref, w_ref in zip(a_refs[1:], w_refs[1:]):
        acc = acc + jnp.dot(a_ref[...], w_ref[...], preferred_element_type=F32)
    o_ref[...] = x_ref[...] + gt_ref[...] * acc


def _outproj(x, mod_l, k_gt, slot0, rows_per_slot, a_list, w):
    m, d = x.shape
    tm = _pick(rows_per_slot, 1024)
    tn = _pick(d, 512)
    in_specs = [pl.BlockSpec((tm, tn), lambda i, j: (i, j)), _mod_spec(k_gt, slot0, rows_per_slot, tm, tn, col=True)]
    args = [x, mod_l]
    k0 = 0
    w_specs = []
    for a in a_list:
        ka = a.shape[1]
        assert k0 % ka == 0
        in_specs.append(pl.BlockSpec((tm, ka), lambda i, j: (i, 0)))
        w_specs.append(pl.BlockSpec((ka, tn), lambda i, j, kb=k0 // ka: (kb, j)))
        k0 += ka
    assert k0 == w.shape[0]
    return pl.pallas_call(
        functools.partial(_outproj_kernel, n_a=len(a_list)),
        grid=(m // tm, d // tn),
        in_specs=in_specs + w_specs,
        out_specs=pl.BlockSpec((tm, tn), lambda i, j: (i, j)),
        out_shape=jax.ShapeDtypeStruct((m, d), F32),
        compiler_params=_params(("parallel", "parallel")),
        name="outproj",
    )(*args, *a_list, *([w] * len(a_list)))


def _na_plan(rows):
    kr = min(NA_WIN_ROWS, rows)
    qr = NA_QROWS if rows % NA_QROWS == 0 else 1
    slab = min(rows, qr + kr)
    starts, ids, layouts = [], [], []
    for b in range(rows // qr):
        ks = int(np.clip(qr * b - kr // 2, 0, rows - slab))
        lay = []
        for r in range(qr * b, qr * b + qr):
            rs = int(np.clip(r - kr // 2, 0, rows - kr))
            assert ks <= rs and rs + kr <= ks + slab
            lay.append((rs - ks, r - ks))
        lay = tuple(lay)
        if lay not in layouts:
            layouts.append(lay)
        starts.append(ks)
        ids.append(layouts.index(lay))
    return kr, qr, slab, np.array([starts, ids], np.int32), tuple(layouts)


def _na_bias_kernel(rpb_ref, o_ref, *, kr, slab, layouts):
    h = pl.program_id(0)
    n_ro, n_co = 2 * NA_WIN_ROWS - 1, 2 * NA_WIN_COLS - 1
    qc = lax.broadcasted_iota(jnp.int32, (GRID_W, GRID_W), 0)
    kc = lax.broadcasted_iota(jnp.int32, (GRID_W, GRID_W), 1)
    dcol = kc - qc + (NA_WIN_COLS - 1)
    cs = jnp.clip(qc - NA_WIN_COLS // 2, 0, GRID_W - NA_WIN_COLS)
    valid = jnp.logical_and(kc >= cs, kc < cs + NA_WIN_COLS)
    tabs = []
    for ro in range(n_ro):
        acc = jnp.zeros((GRID_W, GRID_W), F32)
        for dd in range(n_co):
            acc = jnp.where(dcol == dd, rpb_ref[(h * n_ro + ro) * n_co + dd], acc)
        tabs.append(jnp.where(valid, acc * LOG2E, NEG_BIG))
    masked = jnp.full((GRID_W, GRID_W), NEG_BIG, F32)
    for ti, lay in enumerate(layouts):
        for qi, (win0, qrow) in enumerate(lay):
            for ki in range(slab):
                ro = ki - qrow + (NA_WIN_ROWS - 1)
                tile = tabs[ro] if win0 <= ki < win0 + kr else masked
                o_ref[ti, qi * GRID_W:(qi + 1) * GRID_W, ki * GRID_W:(ki + 1) * GRID_W] = tile


def _na_bias(rpb, plan):
    kr, qr, slab, _, layouts = plan
    nh = rpb.shape[0]
    shape = (len(layouts), qr * GRID_W, slab * GRID_W)
    return pl.pallas_call(
        functools.partial(_na_bias_kernel, kr=kr, slab=slab, layouts=layouts),
        grid=(nh,),
        in_specs=[pl.BlockSpec(memory_space=pltpu.SMEM)],
        out_specs=pl.BlockSpec((None,) + shape, lambda h: (h, 0, 0, 0)),
        out_shape=jax.ShapeDtypeStruct((nh,) + shape, F32),
        compiler_params=_params(("parallel",)),
        name="na_bias",
    )(rpb.reshape(-1))


def _dot_t(a, b):
    return lax.dot_general(a, b, (((1,), (1,)), ((), ())), preferred_element_type=F32)


def _na_kernel(plan_ref, q_ref, k_ref, v_ref, kc_ref, vc_ref, bias_ref, o_ref, *, n_blocks, nq, nk):
    c = HEAD_DIM ** -0.5 * LOG2E
    kc = kc_ref[...]
    vc = vc_ref[...]

    def block(blk, carry):
        q0 = pl.multiple_of(blk * nq, nq)
        k0 = pl.multiple_of(plan_ref[0, blk] * GRID_W, GRID_W)
        q = q_ref[pl.ds(q0, nq), :]
        s = _dot_t(q, k_ref[pl.ds(k0, nk), :]) * c + bias_ref[plan_ref[1, blk]]
        sc = _dot_t(q, kc) * c
        m = jnp.maximum(jnp.max(s, axis=-1, keepdims=True), jnp.max(sc, axis=-1, keepdims=True))
        p = jnp.exp2(s - m)
        pc = jnp.exp2(sc - m)
        inv = 1.0 / (jnp.sum(p, axis=-1, keepdims=True) + jnp.sum(pc, axis=-1, keepdims=True))
        o = (jnp.dot(p.astype(BF16), v_ref[pl.ds(k0, nk), :], preferred_element_type=F32)
             + jnp.dot(pc.astype(BF16), vc, preferred_element_type=F32)) * inv
        o_ref[pl.ds(q0, nq), :] = o.astype(o_ref.dtype)
        return carry

    lax.fori_loop(0, n_blocks, block, 0)


def _na_attention(qkv_x, qkv_c, bias, plan, batch, seq, ctx_len, n_heads):
    _, qr, slab, plan_arr, layouts = plan
    tab = (len(layouts), qr * GRID_W, slab * GRID_W)
    return pl.pallas_call(
        functools.partial(_na_kernel, n_blocks=plan_arr.shape[1], nq=qr * GRID_W, nk=slab * GRID_W),
        grid=(n_heads, batch),
        in_specs=[
            pl.BlockSpec(memory_space=pltpu.SMEM),
            pl.BlockSpec((seq, HEAD_DIM), lambda h, b: (b, h)),
            pl.BlockSpec((seq, HEAD_DIM), lambda h, b: (b, n_heads + h)),
            pl.BlockSpec((seq, HEAD_DIM), lambda h, b: (b, 2 * n_heads + h)),
            pl.BlockSpec((ctx_len, HEAD_DIM), lambda h, b: (b, n_heads + h)),
            pl.BlockSpec((ctx_len, HEAD_DIM), lambda h, b: (b, 2 * n_heads + h)),
            pl.BlockSpec((None,) + tab, lambda h, b: (h, 0, 0, 0)),
        ],
        out_specs=pl.BlockSpec((seq, HEAD_DIM), lambda h, b: (b, h)),
        out_shape=jax.ShapeDtypeStruct((batch * seq, n_heads * HEAD_DIM), BF16),
        compiler_params=_params(("parallel", "parallel")),
        name="na_attn",
    )(jnp.asarray(plan_arr), qkv_x, qkv_x, qkv_x, qkv_c, qkv_c, bias)


def _ctx_attn_kernel(q_ref, k_ref, v_ref, o_ref):
    s = _dot_t(q_ref[...], k_ref[...]) * HEAD_DIM ** -0.5
    p = jnp.exp(s - jnp.max(s, axis=-1, keepdims=True))
    p = p * (1.0 / jnp.sum(p, axis=-1, keepdims=True))
    o_ref[...] = jnp.dot(p.astype(BF16), v_ref[...], preferred_element_type=F32).astype(o_ref.dtype)


def _ctx_attention(qkv_c, batch, ctx_len, n_heads):
    return pl.pallas_call(
        _ctx_attn_kernel,
        grid=(batch, n_heads),
        in_specs=[
            pl.BlockSpec((ctx_len, HEAD_DIM), lambda b, h: (b, h)),
            pl.BlockSpec((ctx_len, HEAD_DIM), lambda b, h: (b, n_heads + h)),
            pl.BlockSpec((ctx_len, HEAD_DIM), lambda b, h: (b, 2 * n_heads + h)),
        ],
        out_specs=pl.BlockSpec((ctx_len, HEAD_DIM), lambda b, h: (b, h)),
        out_shape=jax.ShapeDtypeStruct((batch * ctx_len, n_heads * HEAD_DIM), BF16),
        compiler_params=_params(("parallel", "parallel")),
        name="ctx_attn",
    )(qkv_c, qkv_c, qkv_c)


def _diff_kernel(*refs, n_src, lam_init, src_rows):
    lam_ref, g_ref, q_ref = refs[0], refs[1], refs[2]
    k_refs = refs[3:3 + n_src]
    v_refs = refs[3 + n_src:3 + 2 * n_src]
    o_ref = refs[3 + 2 * n_src]
    acc_ref = refs[4 + 2 * n_src]
    tq = q_ref.shape[0]
    c = HEAD_DIM ** -0.5 * LOG2E
    lf = lam_ref[...]
    lam = (jnp.exp(jnp.sum(lf[0:1] * lf[1:2], axis=-1, keepdims=True))
           - jnp.exp(jnp.sum(lf[2:3] * lf[3:4], axis=-1, keepdims=True)) + lam_init)
    qs = [q_ref[:, mi * HEAD_DIM:(mi + 1) * HEAD_DIM] for mi in range(2)]
    acc_ref[...] = jnp.zeros_like(acc_ref)

    def chunk(carry, k_ref, v_ref, r0, n):
        rows = pl.ds(r0, n)
        v = v_ref[rows, :]
        out = []
        for mi in range(2):
            m, l = carry[2 * mi], carry[2 * mi + 1]
            s = _dot_t(qs[mi], k_ref[rows, mi * HEAD_DIM:(mi + 1) * HEAD_DIM])
            m_new = jnp.maximum(m, jnp.max(s, axis=-1, keepdims=True))
            a = jnp.exp2((m - m_new) * c)
            p = jnp.exp2((s - m_new) * c)
            acc_ref[mi] = acc_ref[mi] * a + jnp.dot(p.astype(BF16), v, preferred_element_type=F32)
            out += [m_new, l * a + jnp.sum(p, axis=-1, keepdims=True)]
        return tuple(out)

    carry = (jnp.full((tq, 1), NEG_BIG, F32), jnp.zeros((tq, 1), F32)) * 2
    for k_ref, v_ref, rows in zip(k_refs, v_refs, src_rows):
        ck = _pick(rows, DIFF_KEY_CHUNK)
        if rows == ck:
            carry = chunk(carry, k_ref, v_ref, 0, ck)
        else:
            carry = lax.fori_loop(
                0, rows // ck,
                lambda i, cr, k_ref=k_ref, v_ref=v_ref, ck=ck: chunk(cr, k_ref, v_ref, pl.multiple_of(i * ck, ck), ck),
                carry, unroll=True)
    _, l1, _, l2 = carry
    o = acc_ref[0] * (1.0 / l1) - acc_ref[1] * (lam / l2)
    ms = jnp.mean(o * o, axis=-1, keepdims=True)
    o_ref[...] = ((o * lax.rsqrt(ms + EPS)) * g_ref[...] * (1.0 - lam_init)).astype(o_ref.dtype)


def _diff_attention(q_arr, q_rows, kv_list, lam_vecs, subln_g, lam_init, batch, n_heads, q_col0, tq_want):
    hw = 2 * HEAD_DIM
    tq = _pick(q_rows, tq_want)
    nq = q_rows // tq
    qb = q_col0 // hw
    in_specs = [
        pl.BlockSpec((4, HEAD_DIM), lambda b, h, i: (0, 0)),
        pl.BlockSpec((1, hw), lambda b, h, i: (0, 0)),
        pl.BlockSpec((tq, hw), lambda b, h, i: (b * nq + i, qb + h)),
    ]
    args = [lam_vecs, subln_g.reshape(1, hw), q_arr]
    for arr, rows in kv_list:
        in_specs.append(pl.BlockSpec((rows, hw), lambda b, h, i: (b, qb + n_heads + h)))
        args.append(arr)
    for arr, rows in kv_list:
        in_specs.append(pl.BlockSpec((rows, hw), lambda b, h, i: (b, qb + 2 * n_heads + h)))
        args.append(arr)
    return pl.pallas_call(
        functools.partial(_diff_kernel, n_src=len(kv_list), lam_init=lam_init,
                          src_rows=tuple(rows for _, rows in kv_list)),
        grid=(batch, n_heads, nq),
        in_specs=in_specs,
        out_specs=pl.BlockSpec((tq, hw), lambda b, h, i: (b * nq + i, h)),
        out_shape=jax.ShapeDtypeStruct((batch * q_rows, n_heads * hw), BF16),
        scratch_shapes=[pltpu.VMEM((2, tq, hw), F32)],
        compiler_params=_params(("parallel", "parallel", "parallel")),
        name="diff_attn",
    )(*args)


def _cumsum_rows(a):
    n = a.shape[0]
    row = lax.broadcasted_iota(jnp.int32, a.shape, 0)
    d = 1
    while d < n:
        a = a + jnp.where(row >= d, pltpu.roll(a, d, 0), 0.0)
        d *= 2
    return a


def _ssd_kernel(*refs, n_ctx_chunks, n_x_chunks, hpg, n_heads, ctx_out):
    (xs_x, z_x, b_x, c_x, dt_x, xs_c, z_c, b_c, c_c, dt_c, bias_ref, alog_ref, dsum_ref, ng_ref) = refs[:14]
    n_out = 2 if ctx_out else 1
    yx_ref = refs[14]
    yc_ref = refs[15] if ctx_out else None
    yacc_x, yacc_c, sc_x, sc_c, rft_x, rft_c, hf_ref, hb_ref = refs[14 + n_out:]
    t = SSM_CHUNK
    pw = 2 * SSM_HEADDIM
    g = pl.program_id(1)
    nl = 2 * n_heads

    lane = lax.broadcasted_iota(jnp.int32, (t, nl), 1)
    is_fwd = lane < n_heads
    bias = bias_ref[...]
    neg_a2 = -jnp.exp(alog_ref[...]) * LOG2E
    li = lax.broadcasted_iota(jnp.int32, (t, t), 0)
    si = lax.broadcasted_iota(jnp.int32, (t, t), 1)
    add_lo = jnp.where(li >= si, 0.0, NEG_BIG)
    add_up = jnp.where(li <= si, 0.0, NEG_BIG)
    pair_lane = lax.broadcasted_iota(jnp.int32, (t, pw), 1) < SSM_HEADDIM
    lane_shift = (nl - g * hpg) % nl

    def pair_weights(xp):
        zero = jnp.zeros_like(xp)
        return jnp.concatenate([jnp.where(pair_lane, xp, zero), jnp.where(pair_lane, zero, xp)], axis=0)

    def col(scl, idx):
        return jnp.broadcast_to(scl[:, idx:idx + 1], (t, t))

    def fwd_chunk(xs_ref, b_ref, c_ref, dt_ref, yacc, sc_ref, rft_ref, r0, h0):
        rows = pl.ds(r0, t)
        dt = jax.nn.softplus(dt_ref[rows, :] + bias)
        a2 = dt * neg_a2
        cs = _cumsum_rows(a2)
        sc2 = jnp.where(is_fwd, cs, cs[t - 1:t, :] - cs + a2)
        sc_ref[rows, :] = sc2
        rft_ref[pl.ds(h0, nl), :] = (sc2 - jnp.log(dt) * LOG2E).T
        scl = pltpu.roll(sc2, lane_shift, 1)
        xb = xs_ref[rows, :]
        bb = b_ref[rows, :]
        cb_ = c_ref[rows, :]
        cbm = _dot_t(cb_, bb)
        bt = bb.astype(F32).T
        yoff = jnp.dot(cb_, hf_ref[...].astype(BF16), preferred_element_type=F32)
        for jp in range(hpg // 2):
            ms, bws, es = [], [], []
            for j in (2 * jp, 2 * jp + 1):
                cf = col(scl, j)
                cbk = col(scl, n_heads + j)
                rf = rft_ref[pl.ds(h0 + g * hpg + j, 1), :]
                rb = rft_ref[pl.ds(h0 + n_heads + g * hpg + j, 1), :]
                ms.append((cbm * (jnp.exp2(cf - rf + add_lo) + jnp.exp2(cbk - rb + add_up))).astype(BF16))
                bws.append((bt * jnp.exp2(cf[t - 1:t, :] - rf)).astype(BF16))
                es.append(jnp.exp2(cf))
            cols = slice(jp * pw, (jp + 1) * pw)
            xp = xb[:, cols]
            lhs = jnp.concatenate([jnp.concatenate(ms, axis=1), jnp.concatenate(bws, axis=1)], axis=0)
            res = jnp.dot(lhs, pair_weights(xp), preferred_element_type=F32)
            e_pair = jnp.where(pair_lane, es[0], es[1])
            yacc[rows, cols] = res[:t] + xp.astype(F32) * dsum_ref[:, cols] + yoff[:, cols] * e_pair
            hf_ref[:, cols] = hf_ref[:, cols] * e_pair[t - 1:t, :] + res[t:]

    def bwd_chunk(xs_ref, z_ref, b_ref, c_ref, yacc, sc_ref, rft_ref, y_ref, r0, h0):
        rows = pl.ds(r0, t)
        scl = pltpu.roll(sc_ref[rows, :], lane_shift, 1)
        xb = xs_ref[rows, :]
        bt = b_ref[rows, :].astype(F32).T
        yoff = jnp.dot(c_ref[rows, :], hb_ref[...].astype(BF16), preferred_element_type=F32)
        ys = []
        for jp in range(hpg // 2):
            bws, es = [], []
            for j in (2 * jp, 2 * jp + 1):
                cbk = col(scl, n_heads + j)
                rb = rft_ref[pl.ds(h0 + n_heads + g * hpg + j, 1), :]
                bws.append((bt * jnp.exp2(cbk[0:1, :] - rb)).astype(BF16))
                es.append(jnp.exp2(cbk))
            cols = slice(jp * pw, (jp + 1) * pw)
            grow = jnp.dot(jnp.concatenate(bws, axis=1), pair_weights(xb[:, cols]), preferred_element_type=F32)
            e_pair = jnp.where(pair_lane, es[0], es[1])
            ys.append(yacc[rows, cols] + yoff[:, cols] * e_pair)
            hb_ref[:, cols] = hb_ref[:, cols] * e_pair[0:1, :] + grow
        if y_ref is not None:
            yg = jnp.concatenate(ys, axis=1) * _silu(z_ref[rows, :].astype(F32))
            ms = jnp.mean(yg * yg, axis=-1, keepdims=True)
            y_ref[rows, :] = ((yg * lax.rsqrt(ms + EPS)) * ng_ref[...]).astype(y_ref.dtype)

    hf_ref[...] = jnp.zeros_like(hf_ref)
    hb_ref[...] = jnp.zeros_like(hb_ref)
    for k in range(n_ctx_chunks):
        fwd_chunk(xs_c, b_c, c_c, dt_c, yacc_c, sc_c, rft_c, k * t, k * nl)

    def fx(k, carry):
        fwd_chunk(xs_x, b_x, c_x, dt_x, yacc_x, sc_x, rft_x, pl.multiple_of(k * t, t), pl.multiple_of(k * nl, nl))
        return carry

    lax.fori_loop(0, n_x_chunks, fx, 0, unroll=SSD_CHUNK_UNROLL)
    for k in reversed(range(n_ctx_chunks)):
        bwd_chunk(xs_c, z_c, b_c, c_c, yacc_c, sc_c, rft_c, yc_ref, k * t, k * nl)

    def bx(k, carry):
        kk = n_x_chunks - 1 - k
        bwd_chunk(xs_x, z_x, b_x, c_x, yacc_x, sc_x, rft_x, yx_ref, pl.multiple_of(kk * t, t),
                  pl.multiple_of(kk * nl, nl))
        return carry

    lax.fori_loop(0, n_x_chunks, bx, 0, unroll=SSD_CHUNK_UNROLL)


def _ssd(zx, zc, dtx, dtc, a_log, dt_bias, d_skip, norm_g, batch, seq, ctx_len, ctx_out):
    inner = norm_g.shape[0]
    n_heads = inner // SSM_HEADDIM
    hpg = n_heads // SSM_GROUPS
    gw = hpg * SSM_HEADDIM
    nl = 2 * n_heads
    assert hpg % 2 == 0 and seq % SSM_CHUNK == 0 and ctx_len % SSM_CHUNK == 0
    assert SSM_CHUNK == 2 * SSM_HEADDIM == SSM_STATE == V7X_LANES and nl == V7X_LANES
    zb = inner // gw
    b0 = (2 * inner) // SSM_STATE
    c0 = b0 + SSM_GROUPS

    def specs(rows):
        return [
            pl.BlockSpec((rows, gw), lambda b, g: (b, zb + g)),
            pl.BlockSpec((rows, gw), lambda b, g: (b, g)),
            pl.BlockSpec((rows, SSM_STATE), lambda b, g: (b, b0 + g)),
            pl.BlockSpec((rows, SSM_STATE), lambda b, g: (b, c0 + g)),
            pl.BlockSpec((rows, nl), lambda b, g: (b, 0)),
        ]

    dsum = jnp.repeat(d_skip[0] + d_skip[1], SSM_HEADDIM).reshape(1, inner).astype(F32)
    in_specs = specs(seq) + specs(ctx_len) + [
        pl.BlockSpec((1, nl), lambda b, g: (0, 0)),
        pl.BlockSpec((1, nl), lambda b, g: (0, 0)),
        pl.BlockSpec((1, gw), lambda b, g: (0, g)),
        pl.BlockSpec((1, gw), lambda b, g: (0, g)),
    ]
    out_specs = [pl.BlockSpec((seq, gw), lambda b, g: (b, g))]
    out_shape = [jax.ShapeDtypeStruct((batch * seq, inner), BF16)]
    if ctx_out:
        out_specs.append(pl.BlockSpec((ctx_len, gw), lambda b, g: (b, g)))
        out_shape.append(jax.ShapeDtypeStruct((batch * ctx_len, inner), BF16))
    ncx, ncc = seq // SSM_CHUNK, ctx_len // SSM_CHUNK
    scratch = [
        pltpu.VMEM((seq, gw), F32), pltpu.VMEM((ctx_len, gw), F32),
        pltpu.VMEM((seq, nl), F32), pltpu.VMEM((ctx_len, nl), F32),
        pltpu.VMEM((ncx * nl, SSM_CHUNK), F32), pltpu.VMEM((ncc * nl, SSM_CHUNK), F32),
        pltpu.VMEM((SSM_STATE, gw), F32), pltpu.VMEM((SSM_STATE, gw), F32),
    ]
    outs = pl.pallas_call(
        functools.partial(_ssd_kernel, n_ctx_chunks=ctx_len // SSM_CHUNK, n_x_chunks=seq // SSM_CHUNK,
                          hpg=hpg, n_heads=n_heads, ctx_out=ctx_out),
        grid=(batch, SSM_GROUPS),
        in_specs=in_specs,
        out_specs=out_specs,
        out_shape=out_shape,
        scratch_shapes=scratch,
        compiler_params=_params(("parallel", "parallel")),
        name="ssd",
    )(zx, zx, zx, zx, dtx, zc, zc, zc, zc, dtc,
      dt_bias.reshape(1, nl), a_log.reshape(1, nl), dsum, norm_g.reshape(1, inner))
    return (outs[0], outs[1]) if ctx_out else (outs[0], None)


def _rope_tables(seq):
    quarter = HEAD_DIM // 4
    inv = 1.0 / (ROPE_THETA ** (jnp.arange(quarter, dtype=F32) / quarter))
    t = jnp.arange(seq)
    row = (t // GRID_W).astype(F32)[:, None] * inv
    col = (t % GRID_W).astype(F32)[:, None] * inv
    ang = jnp.concatenate([row, row, col, col], axis=-1)
    cos, sin = jnp.cos(ang), jnp.sin(ang)
    first = (np.arange(HEAD_DIM) % (2 * quarter)) < quarter
    return cos, jnp.where(first, -sin, 0.0), jnp.where(first, 0.0, sin)


def kernel(x, c, ctx, c_ctx, mod_w, mod_b, norm_g, ffn_w1, ffn_w3, ffn_w2, attn_w_in, attn_w_out, na_rpb,
           diff_lambda, diff_subln_g, ssm_w_in, ssm_conv_w, ssm_conv_b, ssm_a_log, ssm_dt_bias, ssm_d,
           ssm_norm_g, ssm_w_out, final_norm_g):
    batch, seq, d = x.shape
    ctx_len = ctx.shape[1]
    depth = mod_w.shape[0]
    assert batch + 1 <= MOD_SLOTS
    na_w = d // 2
    na_heads = na_w // HEAD_DIM
    diff_heads = na_w // (2 * HEAD_DIM)
    inner = ssm_norm_g.shape[1]
    n_ssm_heads = inner // SSM_HEADDIM

    cvec = jnp.zeros((MOD_SLOTS, d), F32).at[:batch].set(c).at[batch].set(c_ctx)
    mod = _modulation(cvec, mod_w, mod_b)
    mod = mod.reshape(depth, MOD_SLOTS, N_MOD, d).transpose(0, 2, 1, 3)[:, :, :, None, :]

    xs = x.reshape(batch * seq, d)
    hs = ctx.reshape(batch * ctx_len, d)
    rope = _rope_tables(seq)
    na_plan = _na_plan(seq // GRID_W)
    xslot = (0, seq)
    cslot = (batch, batch * ctx_len)
    ffn_w = tuple(w.astype(BF16) for w in (ffn_w1, ffn_w3, ffn_w2))

    for layer in range(depth):
        ctx_out = layer < depth - 1
        last = layer == depth - 1
        ml = mod[layer]
        ng = norm_g[layer]
        xs = _ffn(xs, ml, (0, 1, 2), *xslot, ng[0], *ffn_w, (layer, 0))
        hs = _ffn(hs, ml, (0, 1, 2), *cslot, ng[0], *ffn_w, (layer, 0))

        i = layer // 2
        if layer % 2 == 0:
            lam_init = 0.8 - 0.6 * math.exp(-0.3 * layer)
            w_in = attn_w_in[i].astype(BF16)
            w_out = attn_w_out[i].astype(BF16)
            dq0 = 3 * na_w
            qkv_x = _proj(xs, ml, (3, 4), *xslot, ng[1], w_in, BF16, 1024, rope=(rope, (dq0, dq0 + 2 * na_w)))
            qkv_c = _proj(hs, ml, (3, 4), *cslot, ng[1], w_in, BF16, 1024)
            bias = _na_bias(na_rpb[i], na_plan)
            na_x = _na_attention(qkv_x, qkv_c, bias, na_plan, batch, seq, ctx_len, na_heads)
            df_x = _diff_attention(qkv_x, seq, [(qkv_x, seq), (qkv_c, ctx_len)], diff_lambda[i], diff_subln_g[i],
                                   lam_init, batch, diff_heads, dq0, 512)
            xs = _outproj(xs, ml, 5, *xslot, [na_x, df_x], w_out)
            if ctx_out:
                na_c = _ctx_attention(qkv_c, batch, ctx_len, na_heads)
                df_c = _diff_attention(qkv_c, ctx_len, [(qkv_c, ctx_len)], diff_lambda[i], diff_subln_g[i],
                                       lam_init, batch, diff_heads, dq0, 256)
                hs = _outproj(hs, ml, 5, *cslot, [na_c, df_c], w_out)
        else:
            n_zx = 2 * inner + 2 * SSM_GROUPS * SSM_STATE
            w_in = ssm_w_in[i].astype(BF16)
            dt_cols = (n_zx, w_in.shape[1])
            cw, cb = ssm_conv_w[i], ssm_conv_b[i]
            zx = _proj_conv(xs, ml, (3, 4), *xslot, seq, ng[1], w_in, inner, cw, cb)
            zc = _proj_conv(hs, ml, (3, 4), *cslot, ctx_len, ng[1], w_in, inner, cw, cb)
            dtx = _proj(xs, ml, (3, 4), *xslot, ng[1], w_in, F32, 2 * n_ssm_heads, cols=dt_cols)
            dtc = _proj(hs, ml, (3, 4), *cslot, ng[1], w_in, F32, 2 * n_ssm_heads, cols=dt_cols)
            yx, yc = _ssd(zx, zc, dtx, dtc, ssm_a_log[i], ssm_dt_bias[i], ssm_d[i], ssm_norm_g[i],
                          batch, seq, ctx_len, ctx_out)
            w_out = ssm_w_out[i].astype(BF16)
            xs = _outproj(xs, ml, 5, *xslot, [yx], w_out)
            if ctx_out:
                hs = _outproj(hs, ml, 5, *cslot, [yc], w_out)

        xs = _ffn(xs, ml, (6, 7, 8), *xslot, ng[2], *ffn_w, (layer, 1), final_g=final_norm_g if last else None)
        if ctx_out:
            hs = _ffn(hs, ml, (6, 7, 8), *cslot, ng[2], *ffn_w, (layer, 1))
    return xs.reshape(batch, seq, d)
```

```python
import functools
import math

import jax
import jax.numpy as jnp
import numpy as np
from jax import lax
from jax.experimental import pallas as pl
from jax.experimental.pallas import tpu as pltpu

F32 = jnp.float32
BF16 = jnp.bfloat16

HEAD_DIM = 128
GRID_W = 64
N_MOD = 9
NA_WIN_ROWS = 8
NA_WIN_COLS = 16
ROPE_THETA = 10000.0
SSM_HEADDIM = 64
SSM_STATE = 128
SSM_GROUPS = 8
SSM_CONV = 4
SSM_CHUNK = 128
NEG_BIG = -1e30
EPS = 1e-6

V7X_VMEM_BYTES = 64 * 1024 * 1024
V7X_LANES = 128
BF16_SUBLANES = 16
VMEM_LIMIT = (V7X_VMEM_BYTES * 7) // 8
LOG2E = math.log2(math.e)
NA_QROWS = 4
CONV_ROW_BLOCKS = 3
DIFF_KEY_CHUNK = 2048
SSD_CHUNK_UNROLL = 4
MOD_SLOTS = 16
CONV_HALO = BF16_SUBLANES


def _params(sem, vmem=VMEM_LIMIT):
    return pltpu.CompilerParams(dimension_semantics=sem, vmem_limit_bytes=vmem)


def _pick(n, want):
    t = min(n, want)
    while n % t:
        t -= 1
    return t


def _silu(a):
    return a * jax.nn.sigmoid(a)


def _modulate(x, g, shift, scale):
    ms = jnp.mean(x * x, axis=-1, keepdims=True)
    return (x * lax.rsqrt(ms + EPS)) * (g * (1.0 + scale)) + shift


def _mod_spec(kind, slot0, rows_per_slot, tm, d, col=False):
    if col:
        return pl.BlockSpec((None, None, 1, d), lambda i, j: (kind, slot0 + (i * tm) // rows_per_slot, 0, j))
    return pl.BlockSpec((None, None, 1, d), lambda i, j: (kind, slot0 + (i * tm) // rows_per_slot, 0, 0))


def _mod_kernel(c_ref, w_ref, b_ref, o_ref):
    a = _silu(c_ref[...])
    o_ref[...] = jnp.dot(a, w_ref[...], preferred_element_type=F32, precision=lax.Precision.HIGHEST) + b_ref[...]


def _modulation(cvec, mod_w, mod_b):
    depth, d, n = mod_w.shape
    tn = _pick(n, 1024)
    return pl.pallas_call(
        _mod_kernel,
        grid=(depth, n // tn),
        in_specs=[
            pl.BlockSpec((MOD_SLOTS, d), lambda l, j: (0, 0)),
            pl.BlockSpec((None, d, tn), lambda l, j: (l, 0, j)),
            pl.BlockSpec((None, 1, tn), lambda l, j: (l, 0, j)),
        ],
        out_specs=pl.BlockSpec((None, MOD_SLOTS, tn), lambda l, j: (l, 0, j)),
        out_shape=jax.ShapeDtypeStruct((depth, MOD_SLOTS, n), F32),
        compiler_params=_params(("parallel", "parallel")),
        name="modulation",
    )(cvec, mod_w, mod_b.reshape(depth, 1, n))


def _ffn_kernel(*refs, final):
    if final:
        x_ref, sh_ref, sc_ref, gt_ref, g_ref, w1_ref, w3_ref, w2_ref, fg_ref, o_ref, h_ref = refs
    else:
        x_ref, sh_ref, sc_ref, gt_ref, g_ref, w1_ref, w3_ref, w2_ref, o_ref, h_ref = refs
    j = pl.program_id(1)

    @pl.when(j == 0)
    def _():
        h_ref[...] = _modulate(x_ref[...], g_ref[...], sh_ref[...], sc_ref[...]).astype(BF16)
        o_ref[...] = jnp.zeros_like(o_ref)

    h = h_ref[...]
    half = w1_ref.shape[1] // 2
    acc = None
    for c0 in (0, half):
        a = jnp.dot(h, w1_ref[:, c0:c0 + half], preferred_element_type=F32)
        b = jnp.dot(h, w3_ref[:, c0:c0 + half], preferred_element_type=F32)
        t = jnp.dot((_silu(a) * b).astype(BF16), w2_ref[c0:c0 + half, :], preferred_element_type=F32)
        acc = t if acc is None else acc + t
    o_ref[...] += acc

    @pl.when(j == pl.num_programs(1) - 1)
    def _():
        y = x_ref[...] + (0.5 * gt_ref[...]) * o_ref[...]
        if final:
            ms = jnp.mean(y * y, axis=-1, keepdims=True)
            y = (y * lax.rsqrt(ms + EPS)) * fg_ref[...]
        o_ref[...] = y


def _ffn(x, mod_l, kinds, slot0, rows_per_slot, norm_g, w1, w3, w2, widx, final_g=None):
    m, d = x.shape
    f = w1.shape[-1]
    tm = _pick(rows_per_slot, 512)
    tn = _pick(f, 512)
    nc = f // tn
    final = final_g is not None
    k_sh, k_sc, k_gt = kinds
    la, wh = widx
    in_specs = [
        pl.BlockSpec((tm, d), lambda i, j: (i, 0)),
        _mod_spec(k_sh, slot0, rows_per_slot, tm, d),
        _mod_spec(k_sc, slot0, rows_per_slot, tm, d),
        _mod_spec(k_gt, slot0, rows_per_slot, tm, d),
        pl.BlockSpec((1, d), lambda i, j: (0, 0)),
        pl.BlockSpec((None, None, d, tn), lambda i, j: (la, wh, 0, j)),
        pl.BlockSpec((None, None, d, tn), lambda i, j: (la, wh, 0, j)),
        pl.BlockSpec((None, None, tn, d), lambda i, j: (la, wh, j, 0)),
    ]
    args = [x, mod_l, mod_l, mod_l, norm_g.reshape(1, d), w1, w3, w2]
    if final:
        in_specs.append(pl.BlockSpec((1, d), lambda i, j: (0, 0)))
        args.append(final_g.reshape(1, d))
    return pl.pallas_call(
        functools.partial(_ffn_kernel, final=final),
        grid=(m // tm, nc),
        in_specs=in_specs,
        out_specs=pl.BlockSpec((tm, d), lambda i, j: (i, 0)),
        out_shape=jax.ShapeDtypeStruct((m, d), F32),
        scratch_shapes=[pltpu.VMEM((tm, d), BF16)],
        compiler_params=_params(("parallel", "arbitrary")),
        name="ffn",
    )(*args)


def _rope_tile(res, cos, sin_a, sin_b):
    outs = []
    for k in range(res.shape[1] // HEAD_DIM):
        r = res[:, k * HEAD_DIM:(k + 1) * HEAD_DIM]
        up = pltpu.roll(r, HEAD_DIM - HEAD_DIM // 4, 1)
        dn = pltpu.roll(r, HEAD_DIM // 4, 1)
        outs.append(r * cos + up * sin_a + dn * sin_b)
    return jnp.concatenate(outs, axis=1)


def _proj_kernel(*refs, rope_tiles):
    if rope_tiles:
        x_ref, sh_ref, sc_ref, g_ref, w_ref, cos_ref, sa_ref, sb_ref, o_ref, h_ref = refs
    else:
        x_ref, sh_ref, sc_ref, g_ref, w_ref, o_ref, h_ref = refs
    j = pl.program_id(1)

    @pl.when(j == 0)
    def _():
        h_ref[...] = _modulate(x_ref[...], g_ref[...], sh_ref[...], sc_ref[...]).astype(BF16)

    res = jnp.dot(h_ref[...], w_ref[...], preferred_element_type=F32)
    if rope_tiles:
        is_rope = functools.reduce(jnp.logical_or, [j == t for t in rope_tiles])

        @pl.when(is_rope)
        def _():
            o_ref[...] = _rope_tile(res, cos_ref[...], sa_ref[...], sb_ref[...]).astype(o_ref.dtype)

        @pl.when(jnp.logical_not(is_rope))
        def _():
            o_ref[...] = res.astype(o_ref.dtype)
    else:
        o_ref[...] = res.astype(o_ref.dtype)


def _proj(x, mod_l, kinds, slot0, rows_per_slot, norm_g, w, out_dtype, tn_want, rope=None, cols=None):
    m, d = x.shape
    c0, c1 = cols if cols is not None else (0, w.shape[1])
    n = c1 - c0
    tm = _pick(rows_per_slot, 1024)
    tn = _pick(math.gcd(n, c0) if c0 else n, tn_want)
    cb0 = c0 // tn
    k_sh, k_sc = kinds
    in_specs = [
        pl.BlockSpec((tm, d), lambda i, j: (i, 0)),
        _mod_spec(k_sh, slot0, rows_per_slot, tm, d),
        _mod_spec(k_sc, slot0, rows_per_slot, tm, d),
        pl.BlockSpec((1, d), lambda i, j: (0, 0)),
        pl.BlockSpec((d, tn), lambda i, j: (0, cb0 + j)),
    ]
    args = [x, mod_l, mod_l, norm_g.reshape(1, d), w]
    rope_tiles = ()
    if rope is not None:
        tables, rope_cols = rope
        assert rope_cols[0] % tn == 0 and rope_cols[1] % tn == 0
        rope_tiles = tuple(range(rope_cols[0] // tn, rope_cols[1] // tn))
        tps = rows_per_slot // tm
        for t in tables:
            in_specs.append(pl.BlockSpec((tm, HEAD_DIM), lambda i, j: (i % tps, 0)))
            args.append(t)
    return pl.pallas_call(
        functools.partial(_proj_kernel, rope_tiles=rope_tiles),
        grid=(m // tm, n // tn),
        in_specs=in_specs,
        out_specs=pl.BlockSpec((tm, tn), lambda i, j: (i, j)),
        out_shape=jax.ShapeDtypeStruct((m, n), out_dtype),
        scratch_shapes=[pltpu.VMEM((tm, d), BF16)],
        compiler_params=_params(("parallel", "arbitrary")),
        name="proj",
    )(*args)


def _proj_conv_kernel(xp_ref, x_ref, xn_ref, sh_ref, sc_ref, g_ref, w_ref, cw_ref, cb_ref, o_ref, h_ref,
                      *, tm, n_plain, tiles_per_seq):
    i = pl.program_id(0)
    j = pl.program_id(1)
    n_all = tm + 2 * CONV_HALO

    @pl.when(j == 0)
    def _():
        g, sh, sc = g_ref[...], sh_ref[...], sc_ref[...]
        t = i % tiles_per_seq
        keep_prev = (t != 0).astype(F32)
        keep_next = (t != tiles_per_seq - 1).astype(F32)
        h_ref[0:CONV_HALO, :] = (_modulate(xp_ref[...], g, sh, sc) * keep_prev).astype(BF16)
        h_ref[CONV_HALO:CONV_HALO + tm, :] = _modulate(x_ref[...], g, sh, sc).astype(BF16)
        h_ref[CONV_HALO + tm:n_all, :] = (_modulate(xn_ref[...], g, sh, sc) * keep_next).astype(BF16)

    @pl.when(j < n_plain)
    def _():
        o_ref[...] = jnp.dot(h_ref[CONV_HALO:CONV_HALO + tm, :], w_ref[...],
                             preferred_element_type=F32).astype(o_ref.dtype)

    @pl.when(j >= n_plain)
    def _():
        w = w_ref[...]
        step = n_all // CONV_ROW_BLOCKS if n_all % (CONV_ROW_BLOCKS * BF16_SUBLANES) == 0 else n_all
        u = jnp.concatenate([jnp.dot(h_ref[r0:r0 + step, :], w, preferred_element_type=F32)
                             for r0 in range(0, n_all, step)], axis=0)
        left = (SSM_CONV - 1) // 2
        acc = cb_ref[...] + cw_ref[left:left + 1, :] * u
        for tap in range(SSM_CONV):
            if tap == left:
                continue
            acc = acc + cw_ref[tap:tap + 1, :] * pltpu.roll(u, (left - tap) % n_all, 0)
        y = acc[CONV_HALO:CONV_HALO + tm, :]
        o_ref[...] = _silu(y).astype(o_ref.dtype)


def _proj_conv(x, mod_l, kinds, slot0, rows_per_slot, rows_per_seq, norm_g, w, n_plain_cols, conv_w, conv_b):
    m, d = x.shape
    n = n_plain_cols + conv_w.shape[1]
    tm = _pick(rows_per_seq, 1024)
    tn = _pick(math.gcd(n_plain_cols, n - n_plain_cols), 512)
    n_plain = n_plain_cols // tn
    tps = rows_per_seq // tm
    k_sh, k_sc = kinds
    hb = tm // CONV_HALO
    last_hb = m // CONV_HALO - 1
    in_specs = [
        pl.BlockSpec((CONV_HALO, d), lambda i, j: (jnp.maximum(i * hb - 1, 0), 0)),
        pl.BlockSpec((tm, d), lambda i, j: (i, 0)),
        pl.BlockSpec((CONV_HALO, d), lambda i, j: (jnp.minimum((i + 1) * hb, last_hb), 0)),
        _mod_spec(k_sh, slot0, rows_per_slot, tm, d),
        _mod_spec(k_sc, slot0, rows_per_slot, tm, d),
        pl.BlockSpec((1, d), lambda i, j: (0, 0)),
        pl.BlockSpec((d, tn), lambda i, j: (0, j)),
        pl.BlockSpec((SSM_CONV, tn), lambda i, j: (0, jnp.maximum(j - n_plain, 0))),
        pl.BlockSpec((1, tn), lambda i, j: (0, jnp.maximum(j - n_plain, 0))),
    ]
    return pl.pallas_call(
        functools.partial(_proj_conv_kernel, tm=tm, n_plain=n_plain, tiles_per_seq=tps),
        grid=(m // tm, n // tn),
        in_specs=in_specs,
        out_specs=pl.BlockSpec((tm, tn), lambda i, j: (i, j)),
        out_shape=jax.ShapeDtypeStruct((m, n), BF16),
        scratch_shapes=[pltpu.VMEM((tm + 2 * CONV_HALO, d), BF16)],
        compiler_params=_params(("parallel", "arbitrary")),
        name="proj_conv",
    )(x, x, x, mod_l, mod_l, norm_g.reshape(1, d), w, conv_w, conv_b.reshape(1, -1))


def _outproj_kernel(*refs, n_a):
    x_ref, gt_ref = refs[0], refs[1]
    a_refs = refs[2:2 + n_a]
    w_refs = refs[2 + n_a:2 + 2 * n_a]
    o_ref = refs[2 + 2 * n_a]
    acc = jnp.dot(a_refs[0][...], w_refs[0][...], preferred_element_type=F32)
    for a_ref, w_ref in zip(a_refs[1:], w_refs[1:]):
        acc = acc + jnp.dot(a_ref[...], w_ref[...], preferred_element_type=F32)
    o_ref[...] = x_ref[...] + gt_ref[...] * acc


def _outproj(x, mod_l, k_gt, slot0, rows_per_slot, a_list, w):
    m, d = x.shape
    tm = _pick(rows_per_slot, 1024)
    tn = _pick(d, 512)
    in_specs = [pl.BlockSpec((tm, tn), lambda i, j: (i, j)), _mod_spec(k_gt, slot0, rows_per_slot, tm, tn, col=True)]
    args = [x, mod_l]
    k0 = 0
    w_specs = []
    for a in a_list:
        ka = a.shape[1]
        assert k0 % ka == 0
        in_specs.append(pl.BlockSpec((tm, ka), lambda i, j: (i, 0)))
        w_specs.append(pl.BlockSpec((ka, tn), lambda i, j, kb=k0 // ka: (kb, j)))
        k0 += ka
    assert k0 == w.shape[0]
    return pl.pallas_call(
        functools.partial(_outproj_kernel, n_a=len(a_list)),
        grid=(m // tm, d // tn),
        in_specs=in_specs + w_specs,
        out_specs=pl.BlockSpec((tm, tn), lambda i, j: (i, j)),
        out_shape=jax.ShapeDtypeStruct((m, d), F32),
        compiler_params=_params(("parallel", "parallel")),
        name="outproj",
    )(*args, *a_list, *([w] * len(a_list)))


def _na_plan(rows):
    kr = min(NA_WIN_ROWS, rows)
    qr = NA_QROWS if rows % NA_QROWS == 0 else 1
    slab = min(rows, qr + kr)
    starts, ids, layouts = [], [], []
    for b in range(rows // qr):
        ks = int(np.clip(qr * b - kr // 2, 0, rows - slab))
        lay = []
        for r in range(qr * b, qr * b + qr):
            rs = int(np.clip(r - kr // 2, 0, rows - kr))
            assert ks <= rs and rs + kr <= ks + slab
            lay.append((rs - ks, r - ks))
        lay = tuple(lay)
        if lay not in layouts:
            layouts.append(lay)
        starts.append(ks)
        ids.append(layouts.index(lay))
    return kr, qr, slab, np.array([starts, ids], np.int32), tuple(layouts)


def _na_bias_kernel(rpb_ref, o_ref, *, kr, slab, layouts):
    h = pl.program_id(0)
    n_ro, n_co = 2 * NA_WIN_ROWS - 1, 2 * NA_WIN_COLS - 1
    qc = lax.broadcasted_iota(jnp.int32, (GRID_W, GRID_W), 0)
    kc = lax.broadcasted_iota(jnp.int32, (GRID_W, GRID_W), 1)
    dcol = kc - qc + (NA_WIN_COLS - 1)
    cs = jnp.clip(qc - NA_WIN_COLS // 2, 0, GRID_W - NA_WIN_COLS)
    valid = jnp.logical_and(kc >= cs, kc < cs + NA_WIN_COLS)
    tabs = []
    for ro in range(n_ro):
        acc = jnp.zeros((GRID_W, GRID_W), F32)
        for dd in range(n_co):
            acc = jnp.where(dcol == dd, rpb_ref[(h * n_ro + ro) * n_co + dd], acc)
        tabs.append(jnp.where(valid, acc * LOG2E, NEG_BIG))
    masked = jnp.full((GRID_W, GRID_W), NEG_BIG, F32)
    for ti, lay in enumerate(layouts):
        for qi, (win0, qrow) in enumerate(lay):
            for ki in range(slab):
                ro = ki - qrow + (NA_WIN_ROWS - 1)
                tile = tabs[ro] if win0 <= ki < win0 + kr else masked
                o_ref[ti, qi * GRID_W:(qi + 1) * GRID_W, ki * GRID_W:(ki + 1) * GRID_W] = tile


def _na_bias(rpb, plan):
    kr, qr, slab, _, layouts = plan
    nh = rpb.shape[0]
    shape = (len(layouts), qr * GRID_W, slab * GRID_W)
    return pl.pallas_call(
        functools.partial(_na_bias_kernel, kr=kr, slab=slab, layouts=layouts),
        grid=(nh,),
        in_specs=[pl.BlockSpec(memory_space=pltpu.SMEM)],
        out_specs=pl.BlockSpec((None,) + shape, lambda h: (h, 0, 0, 0)),
        out_shape=jax.ShapeDtypeStruct((nh,) + shape, F32),
        compiler_params=_params(("parallel",)),
        name="na_bias",
    )(rpb.reshape(-1))


def _dot_t(a, b):
    return lax.dot_general(a, b, (((1,), (1,)), ((), ())), preferred_element_type=F32)


def _na_kernel(plan_ref, q_ref, k_ref, v_ref, kc_ref, vc_ref, bias_ref, o_ref, *, n_blocks, nq, nk):
    c = HEAD_DIM ** -0.5 * LOG2E
    kc = kc_ref[...]
    vc = vc_ref[...]

    def block(blk, carry):
        q0 = pl.multiple_of(blk * nq, nq)
        k0 = pl.multiple_of(plan_ref[0, blk] * GRID_W, GRID_W)
        q = q_ref[pl.ds(q0, nq), :]
        s = _dot_t(q, k_ref[pl.ds(k0, nk), :]) * c + bias_ref[plan_ref[1, blk]]
        sc = _dot_t(q, kc) * c
        m = jnp.maximum(jnp.max(s, axis=-1, keepdims=True), jnp.max(sc, axis=-1, keepdims=True))
        p = jnp.exp2(s - m)
        pc = jnp.exp2(sc - m)
        inv = 1.0 / (jnp.sum(p, axis=-1, keepdims=True) + jnp.sum(pc, axis=-1, keepdims=True))
        o = (jnp.dot(p.astype(BF16), v_ref[pl.ds(k0, nk), :], preferred_element_type=F32)
             + jnp.dot(pc.astype(BF16), vc, preferred_element_type=F32)) * inv
        o_ref[pl.ds(q0, nq), :] = o.astype(o_ref.dtype)
        return carry

    lax.fori_loop(0, n_blocks, block, 0)


def _na_attention(qkv_x, qkv_c, bias, plan, batch, seq, ctx_len, n_heads):
    _, qr, slab, plan_arr, layouts = plan
    tab = (len(layouts), qr * GRID_W, slab * GRID_W)
    return pl.pallas_call(
        functools.partial(_na_kernel, n_blocks=plan_arr.shape[1], nq=qr * GRID_W, nk=slab * GRID_W),
        grid=(n_heads, batch),
        in_specs=[
            pl.BlockSpec(memory_space=pltpu.SMEM),
            pl.BlockSpec((seq, HEAD_DIM), lambda h, b: (b, h)),
            pl.BlockSpec((seq, HEAD_DIM), lambda h, b: (b, n_heads + h)),
            pl.BlockSpec((seq, HEAD_DIM), lambda h, b: (b, 2 * n_heads + h)),
            pl.BlockSpec((ctx_len, HEAD_DIM), lambda h, b: (b, n_heads + h)),
            pl.BlockSpec((ctx_len, HEAD_DIM), lambda h, b: (b, 2 * n_heads + h)),
            pl.BlockSpec((None,) + tab, lambda h, b: (h, 0, 0, 0)),
        ],
        out_specs=pl.BlockSpec((seq, HEAD_DIM), lambda h, b: (b, h)),
        out_shape=jax.ShapeDtypeStruct((batch * seq, n_heads * HEAD_DIM), BF16),
        compiler_params=_params(("parallel", "parallel")),
        name="na_attn",
    )(jnp.asarray(plan_arr), qkv_x, qkv_x, qkv_x, qkv_c, qkv_c, bias)


def _ctx_attn_kernel(q_ref, k_ref, v_ref, o_ref):
    s = _dot_t(q_ref[...], k_ref[...]) * HEAD_DIM ** -0.5
    p = jnp.exp(s - jnp.max(s, axis=-1, keepdims=True))
    p = p * (1.0 / jnp.sum(p, axis=-1, keepdims=True))
    o_ref[...] = jnp.dot(p.astype(BF16), v_ref[...], preferred_element_type=F32).astype(o_ref.dtype)


def _ctx_attention(qkv_c, batch, ctx_len, n_heads):
    return pl.pallas_call(
        _ctx_attn_kernel,
        grid=(batch, n_heads),
        in_specs=[
            pl.BlockSpec((ctx_len, HEAD_DIM), lambda b, h: (b, h)),
            pl.BlockSpec((ctx_len, HEAD_DIM), lambda b, h: (b, n_heads + h)),
            pl.BlockSpec((ctx_len, HEAD_DIM), lambda b, h: (b, 2 * n_heads + h)),
        ],
        out_specs=pl.BlockSpec((ctx_len, HEAD_DIM), lambda b, h: (b, h)),
        out_shape=jax.ShapeDtypeStruct((batch * ctx_len, n_heads * HEAD_DIM), BF16),
        compiler_params=_params(("parallel", "parallel")),
        name="ctx_attn",
    )(qkv_c, qkv_c, qkv_c)


def _diff_kernel(*refs, n_src, lam_init, src_rows):
    lam_ref, g_ref, q_ref = refs[0], refs[1], refs[2]
    k_refs = refs[3:3 + n_src]
    v_refs = refs[3 + n_src:3 + 2 * n_src]
    o_ref = refs[3 + 2 * n_src]
    acc_ref = refs[4 + 2 * n_src]
    tq = q_ref.shape[0]
    c = HEAD_DIM ** -0.5 * LOG2E
    lf = lam_ref[...]
    lam = (jnp.exp(jnp.sum(lf[0:1] * lf[1:2], axis=-1, keepdims=True))
           - jnp.exp(jnp.sum(lf[2:3] * lf[3:4], axis=-1, keepdims=True)) + lam_init)
    qs = [q_ref[:, mi * HEAD_DIM:(mi + 1) * HEAD_DIM] for mi in range(2)]
    acc_ref[...] = jnp.zeros_like(acc_ref)

    def chunk(carry, k_ref, v_ref, r0, n):
        rows = pl.ds(r0, n)
        v = v_ref[rows, :]
        out = []
        for mi in range(2):
            m, l = carry[2 * mi], carry[2 * mi + 1]
            s = _dot_t(qs[mi], k_ref[rows, mi * HEAD_DIM:(mi + 1) * HEAD_DIM])
            m_new = jnp.maximum(m, jnp.max(s, axis=-1, keepdims=True))
            a = jnp.exp2((m - m_new) * c)
            p = jnp.exp2((s - m_new) * c)
            acc_ref[mi] = acc_ref[mi] * a + jnp.dot(p.astype(BF16), v, preferred_element_type=F32)
            out += [m_new, l * a + jnp.sum(p, axis=-1, keepdims=True)]
        return tuple(out)

    carry = (jnp.full((tq, 1), NEG_BIG, F32), jnp.zeros((tq, 1), F32)) * 2
    for k_ref, v_ref, rows in zip(k_refs, v_refs, src_rows):
        ck = _pick(rows, DIFF_KEY_CHUNK)
        if rows == ck:
            carry = chunk(carry, k_ref, v_ref, 0, ck)
        else:
            carry = lax.fori_loop(
                0, rows // ck,
                lambda i, cr, k_ref=k_ref, v_ref=v_ref, ck=ck: chunk(cr, k_ref, v_ref, pl.multiple_of(i * ck, ck), ck),
                carry, unroll=True)
    _, l1, _, l2 = carry
    o = acc_ref[0] * (1.0 / l1) - acc_ref[1] * (lam / l2)
    ms = jnp.mean(o * o, axis=-1, keepdims=True)
    o_ref[...] = ((o * lax.rsqrt(ms + EPS)) * g_ref[...] * (1.0 - lam_init)).astype(o_ref.dtype)


def _diff_attention(q_arr, q_rows, kv_list, lam_vecs, subln_g, lam_init, batch, n_heads, q_col0, tq_want):
    hw = 2 * HEAD_DIM
    tq = _pick(q_rows, tq_want)
    nq = q_rows // tq
    qb = q_col0 // hw
    in_specs = [
        pl.BlockSpec((4, HEAD_DIM), lambda b, h, i: (0, 0)),
        pl.BlockSpec((1, hw), lambda b, h, i: (0, 0)),
        pl.BlockSpec((tq, hw), lambda b, h, i: (b * nq + i, qb + h)),
    ]
    args = [lam_vecs, subln_g.reshape(1, hw), q_arr]
    for arr, rows in kv_list:
        in_specs.append(pl.BlockSpec((rows, hw), lambda b, h, i: (b, qb + n_heads + h)))
        args.append(arr)
    for arr, rows in kv_list:
        in_specs.append(pl.BlockSpec((rows, hw), lambda b, h, i: (b, qb + 2 * n_heads + h)))
        args.append(arr)
    return pl.pallas_call(
        functools.partial(_diff_kernel, n_src=len(kv_list), lam_init=lam_init,
                          src_rows=tuple(rows for _, rows in kv_list)),
        grid=(batch, n_heads, nq),
        in_specs=in_specs,
        out_specs=pl.BlockSpec((tq, hw), lambda b, h, i: (b * nq + i, h)),
        out_shape=jax.ShapeDtypeStruct((batch * q_rows, n_heads * hw), BF16),
        scratch_shapes=[pltpu.VMEM((2, tq, hw), F32)],
        compiler_params=_params(("parallel", "parallel", "parallel")),
        name="diff_attn",
    )(*args)


def _cumsum_rows(a):
    n = a.shape[0]
    row = lax.broadcasted_iota(jnp.int32, a.shape, 0)
    d = 1
    while d < n:
        a = a + jnp.where(row >= d, pltpu.roll(a, d, 0), 0.0)
        d *= 2
    return a


def _ssd_kernel(*refs, n_ctx_chunks, n_x_chunks, hpg, n_heads, ctx_out):
    (xs_x, z_x, b_x, c_x, dt_x, xs_c, z_c, b_c, c_c, dt_c, bias_ref, alog_ref, dsum_ref, ng_ref) = refs[:14]
    n_out = 2 if ctx_out else 1
    yx_ref = refs[14]
    yc_ref = refs[15] if ctx_out else None
    yacc_x, yacc_c, sc_x, sc_c, rft_x, rft_c, hf_ref, hb_ref = refs[14 + n_out:]
    t = SSM_CHUNK
    pw = 2 * SSM_HEADDIM
    g = pl.program_id(1)
    nl = 2 * n_heads

    lane = lax.broadcasted_iota(jnp.int32, (t, nl), 1)
    is_fwd = lane < n_heads
    bias = bias_ref[...]
    neg_a2 = -jnp.exp(alog_ref[...]) * LOG2E
    li = lax.broadcasted_iota(jnp.int32, (t, t), 0)
    si = lax.broadcasted_iota(jnp.int32, (t, t), 1)
    add_lo = jnp.where(li >= si, 0.0, NEG_BIG)
    add_up = jnp.where(li <= si, 0.0, NEG_BIG)
    pair_lane = lax.broadcasted_iota(jnp.int32, (t, pw), 1) < SSM_HEADDIM
    lane_shift = (nl - g * hpg) % nl

    def pair_weights(xp):
        zero = jnp.zeros_like(xp)
        return jnp.concatenate([jnp.where(pair_lane, xp, zero), jnp.where(pair_lane, zero, xp)], axis=0)

    def col(scl, idx):
        return jnp.broadcast_to(scl[:, idx:idx + 1], (t, t))

    def fwd_chunk(xs_ref, b_ref, c_ref, dt_ref, yacc, sc_ref, rft_ref, r0, h0):
        rows = pl.ds(r0, t)
        dt = jax.nn.softplus(dt_ref[rows, :] + bias)
        a2 = dt * neg_a2
        cs = _cumsum_rows(a2)
        sc2 = jnp.where(is_fwd, cs, cs[t - 1:t, :] - cs + a2)
        sc_ref[rows, :] = sc2
        rft_ref[pl.ds(h0, nl), :] = (sc2 - jnp.log(dt) * LOG2E).T
        scl = pltpu.roll(sc2, lane_shift, 1)
        xb = xs_ref[rows, :]
        bb = b_ref[rows, :]
        cb_ = c_ref[rows, :]
        cbm = _dot_t(cb_, bb)
        bt = bb.astype(F32).T
        yoff = jnp.dot(cb_, hf_ref[...].astype(BF16), preferred_element_type=F32)
        for jp in range(hpg // 2):
            ms, bws, es = [], [], []
            for j in (2 * jp, 2 * jp + 1):
                cf = col(scl, j)
                cbk = col(scl, n_heads + j)
                rf = rft_ref[pl.ds(h0 + g * hpg + j, 1), :]
                rb = rft_ref[pl.ds(h0 + n_heads + g * hpg + j, 1), :]
                ms.append((cbm * (jnp.exp2(cf - rf + add_lo) + jnp.exp2(cbk - rb + add_up))).astype(BF16))
                bws.append((bt * jnp.exp2(cf[t - 1:t, :] - rf)).astype(BF16))
                es.append(jnp.exp2(cf))
            cols = slice(jp * pw, (jp + 1) * pw)
            xp = xb[:, cols]
            lhs = jnp.concatenate([jnp.concatenate(ms, axis=1), jnp.concatenate(bws, axis=1)], axis=0)
            res = jnp.dot(lhs, pair_weights(xp), preferred_element_type=F32)
            e_pair = jnp.where(pair_lane, es[0], es[1])
            yacc[rows, cols] = res[:t] + xp.astype(F32) * dsum_ref[:, cols] + yoff[:, cols] * e_pair
            hf_ref[:, cols] = hf_ref[:, cols] * e_pair[t - 1:t, :] + res[t:]

    def bwd_chunk(xs_ref, z_ref, b_ref, c_ref, yacc, sc_ref, rft_ref, y_ref, r0, h0):
        rows = pl.ds(r0, t)
        scl = pltpu.roll(sc_ref[rows, :], lane_shift, 1)
        xb = xs_ref[rows, :]
        bt = b_ref[rows, :].astype(F32).T
        yoff = jnp.dot(c_ref[rows, :], hb_ref[...].astype(BF16), preferred_element_type=F32)
        ys = []
        for jp in range(hpg // 2):
            bws, es = [], []
            for j in (2 * jp, 2 * jp + 1):
                cbk = col(scl, n_heads + j)
                rb = rft_ref[pl.ds(h0 + n_heads + g * hpg + j, 1), :]
                bws.append((bt * jnp.exp2(cbk[0:1, :] - rb)).astype(BF16))
                es.append(jnp.exp2(cbk))
            cols = slice(jp * pw, (jp + 1) * pw)
            grow = jnp.dot(jnp.concatenate(bws, axis=1), pair_weights(xb[:, cols]), preferred_element_type=F32)
            e_pair = jnp.where(pair_lane, es[0], es[1])
            ys.append(yacc[rows, cols] + yoff[:, cols] * e_pair)
            hb_ref[:, cols] = hb_ref[:, cols] * e_pair[0:1, :] + grow
        if y_ref is not None:
            yg = jnp.concatenate(ys, axis=1) * _silu(z_ref[rows, :].astype(F32))
            ms = jnp.mean(yg * yg, axis=-1, keepdims=True)
            y_ref[rows, :] = ((yg * lax.rsqrt(ms + EPS)) * ng_ref[...]).astype(y_ref.dtype)

    hf_ref[...] = jnp.zeros_like(hf_ref)
    hb_ref[...] = jnp.zeros_like(hb_ref)
    for k in range(n_ctx_chunks):
        fwd_chunk(xs_c, b_c, c_c, dt_c, yacc_c, sc_c, rft_c, k * t, k * nl)

    def fx(k, carry):
        fwd_chunk(xs_x, b_x, c_x, dt_x, yacc_x, sc_x, rft_x, pl.multiple_of(k * t, t), pl.multiple_of(k * nl, nl))
        return carry

    lax.fori_loop(0, n_x_chunks, fx, 0, unroll=SSD_CHUNK_UNROLL)
    for k in reversed(range(n_ctx_chunks)):
        bwd_chunk(xs_c, z_c, b_c, c_c, yacc_c, sc_c, rft_c, yc_ref, k * t, k * nl)

    def bx(k, carry):
        kk = n_x_chunks - 1 - k
        bwd_chunk(xs_x, z_x, b_x, c_x, yacc_x, sc_x, rft_x, yx_ref, pl.multiple_of(kk * t, t),
                  pl.multiple_of(kk * nl, nl))
        return carry

    lax.fori_loop(0, n_x_chunks, bx, 0, unroll=SSD_CHUNK_UNROLL)


def _ssd(zx, zc, dtx, dtc, a_log, dt_bias, d_skip, norm_g, batch, seq, ctx_len, ctx_out):
    inner = norm_g.shape[0]
    n_heads = inner // SSM_HEADDIM
    hpg = n_heads // SSM_GROUPS
    gw = hpg * SSM_HEADDIM
    nl = 2 * n_heads
    assert hpg % 2 == 0 and seq % SSM_CHUNK == 0 and ctx_len % SSM_CHUNK == 0
    assert SSM_CHUNK == 2 * SSM_HEADDIM == SSM_STATE == V7X_LANES and nl == V7X_LANES
    zb = inner // gw
    b0 = (2 * inner) // SSM_STATE
    c0 = b0 + SSM_GROUPS

    def specs(rows):
        return [
            pl.BlockSpec((rows, gw), lambda b, g: (b, zb + g)),
            pl.BlockSpec((rows, gw), lambda b, g: (b, g)),
            pl.BlockSpec((rows, SSM_STATE), lambda b, g: (b, b0 + g)),
            pl.BlockSpec((rows, SSM_STATE), lambda b, g: (b, c0 + g)),
            pl.BlockSpec((rows, nl), lambda b, g: (b, 0)),
        ]

    dsum = jnp.repeat(d_skip[0] + d_skip[1], SSM_HEADDIM).reshape(1, inner).astype(F32)
    in_specs = specs(seq) + specs(ctx_len) + [
        pl.BlockSpec((1, nl), lambda b, g: (0, 0)),
        pl.BlockSpec((1, nl), lambda b, g: (0, 0)),
        pl.BlockSpec((1, gw), lambda b, g: (0, g)),
        pl.BlockSpec((1, gw), lambda b, g: (0, g)),
    ]
    out_specs = [pl.BlockSpec((seq, gw), lambda b, g: (b, g))]
    out_shape = [jax.ShapeDtypeStruct((batch * seq, inner), BF16)]
    if ctx_out:
        out_specs.append(pl.BlockSpec((ctx_len, gw), lambda b, g: (b, g)))
        out_shape.append(jax.ShapeDtypeStruct((batch * ctx_len, inner), BF16))
    ncx, ncc = seq // SSM_CHUNK, ctx_len // SSM_CHUNK
    scratch = [
        pltpu.VMEM((seq, gw), F32), pltpu.VMEM((ctx_len, gw), F32),
        pltpu.VMEM((seq, nl), F32), pltpu.VMEM((ctx_len, nl), F32),
        pltpu.VMEM((ncx * nl, SSM_CHUNK), F32), pltpu.VMEM((ncc * nl, SSM_CHUNK), F32),
        pltpu.VMEM((SSM_STATE, gw), F32), pltpu.VMEM((SSM_STATE, gw), F32),
    ]
    outs = pl.pallas_call(
        functools.partial(_ssd_kernel, n_ctx_chunks=ctx_len // SSM_CHUNK, n_x_chunks=seq // SSM_CHUNK,
                          hpg=hpg, n_heads=n_heads, ctx_out=ctx_out),
        grid=(batch, SSM_GROUPS),
        in_specs=in_specs,
        out_specs=out_specs,
        out_shape=out_shape,
        scratch_shapes=scratch,
        compiler_params=_params(("parallel", "parallel")),
        name="ssd",
    )(zx, zx, zx, zx, dtx, zc, zc, zc, zc, dtc,
      dt_bias.reshape(1, nl), a_log.reshape(1, nl), dsum, norm_g.reshape(1, inner))
    return (outs[0], outs[1]) if ctx_out else (outs[0], None)


def _rope_tables(seq):
    quarter = HEAD_DIM // 4
    inv = 1.0 / (ROPE_THETA ** (jnp.arange(quarter, dtype=F32) / quarter))
    t = jnp.arange(seq)
    row = (t // GRID_W).astype(F32)[:, None] * inv
    col = (t % GRID_W).astype(F32)[:, None] * inv
    ang = jnp.concatenate([row, row, col, col], axis=-1)
    cos, sin = jnp.cos(ang), jnp.sin(ang)
    first = (np.arange(HEAD_DIM) % (2 * quarter)) < quarter
    return cos, jnp.where(first, -sin, 0.0), jnp.where(first, 0.0, sin)


def kernel(x, c, ctx, c_ctx, mod_w, mod_b, norm_g, ffn_w1, ffn_w3, ffn_w2, attn_w_in, attn_w_out, na_rpb,
           diff_lambda, diff_subln_g, ssm_w_in, ssm_conv_w, ssm_conv_b, ssm_a_log, ssm_dt_bias, ssm_d,
           ssm_norm_g, ssm_w_out, final_norm_g):
    batch, seq, d = x.shape
    ctx_len = ctx.shape[1]
    depth = mod_w.shape[0]
    assert batch + 1 <= MOD_SLOTS
    na_w = d // 2
    na_heads = na_w // HEAD_DIM
    diff_heads = na_w // (2 * HEAD_DIM)
    inner = ssm_norm_g.shape[1]
    n_ssm_heads = inner // SSM_HEADDIM

    cvec = jnp.zeros((MOD_SLOTS, d), F32).at[:batch].set(c).at[batch].set(c_ctx)
    mod = _modulation(cvec, mod_w, mod_b)
    mod = mod.reshape(depth, MOD_SLOTS, N_MOD, d).transpose(0, 2, 1, 3)[:, :, :, None, :]

    xs = x.reshape(batch * seq, d)
    hs = ctx.reshape(batch * ctx_len, d)
    rope = _rope_tables(seq)
    na_plan = _na_plan(seq // GRID_W)
    xslot = (0, seq)
    cslot = (batch, batch * ctx_len)
    ffn_w = tuple(w.astype(BF16) for w in (ffn_w1, ffn_w3, ffn_w2))

    for layer in range(depth):
        ctx_out = layer < depth - 1
        last = layer == depth - 1
        ml = mod[layer]
        ng = norm_g[layer]
        xs = _ffn(xs, ml, (0, 1, 2), *xslot, ng[0], *ffn_w, (layer, 0))
        hs = _ffn(hs, ml, (0, 1, 2), *cslot, ng[0], *ffn_w, (layer, 0))

        i = layer // 2
        if layer % 2 == 0:
            lam_init = 0.8 - 0.6 * math.exp(-0.3 * layer)
            w_in = attn_w_in[i].astype(BF16)
            w_out = attn_w_out[i].astype(BF16)
            dq0 = 3 * na_w
            qkv_x = _proj(xs, ml, (3, 4), *xslot, ng[1], w_in, BF16, 1024, rope=(rope, (dq0, dq0 + 2 * na_w)))
            qkv_c = _proj(hs, ml, (3, 4), *cslot, ng[1], w_in, BF16, 1024)
            bias = _na_bias(na_rpb[i], na_plan)
            na_x = _na_attention(qkv_x, qkv_c, bias, na_plan, batch, seq, ctx_len, na_heads)
            df_x = _diff_attention(qkv_x, seq, [(qkv_x, seq), (qkv_c, ctx_len)], diff_lambda[i], diff_subln_g[i],
                                   lam_init, batch, diff_heads, dq0, 512)
            xs = _outproj(xs, ml, 5, *xslot, [na_x, df_x], w_out)
            if ctx_out:
                na_c = _ctx_attention(qkv_c, batch, ctx_len, na_heads)
                df_c = _diff_attention(qkv_c, ctx_len, [(qkv_c, ctx_len)], diff_lambda[i], diff_subln_g[i],
                                       lam_init, batch, diff_heads, dq0, 256)
                hs = _outproj(hs, ml, 5, *cslot, [na_c, df_c], w_out)
        else:
            n_zx = 2 * inner + 2 * SSM_GROUPS * SSM_STATE
            w_in = ssm_w_in[i].astype(BF16)
            dt_cols = (n_zx, w_in.shape[1])
            cw, cb = ssm_conv_w[i], ssm_conv_b[i]
            zx = _proj_conv(xs, ml, (3, 4), *xslot, seq, ng[1], w_in, inner, cw, cb)
            zc = _proj_conv(hs, ml, (3, 4), *cslot, ctx_len, ng[1], w_in, inner, cw, cb)
            dtx = _proj(xs, ml, (3, 4), *xslot, ng[1], w_in, F32, 2 * n_ssm_heads, cols=dt_cols)
            dtc = _proj(hs, ml, (3, 4), *cslot, ng[1], w_in, F32, 2 * n_ssm_heads, cols=dt_cols)
            yx, yc = _ssd(zx, zc, dtx, dtc, ssm_a_log[i], ssm_dt_bias[i], ssm_d[i], ssm_norm_g[i],
                          batch, seq, ctx_len, ctx_out)
            w_out = ssm_w_out[i].astype(BF16)
            xs = _outproj(xs, ml, 5, *xslot, [yx], w_out)
            if ctx_out:
                hs = _outproj(hs, ml, 5, *cslot, [yc], w_out)

        xs = _ffn(xs, ml, (6, 7, 8), *xslot, ng[2], *ffn_w, (layer, 1), final_g=final_norm_g if last else None)
        if ctx_out:
            hs = _ffn(hs, ml, (6, 7, 8), *cslot, ng[2], *ffn_w, (layer, 1))
    return xs.reshape(batch, seq, d)
```

```python
import functools
import math

import jax
import jax.numpy as jnp
import numpy as np
from jax import lax
from jax.experimental import pallas as pl
from jax.experimental.pallas import tpu as pltpu

F32 = jnp.float32
BF16 = jnp.bfloat16

HEAD_DIM = 128
GRID_W = 64
N_MOD = 9
NA_WIN_ROWS = 8
NA_WIN_COLS = 16
ROPE_THETA = 10000.0
SSM_HEADDIM = 64
SSM_STATE = 128
SSM_GROUPS = 8
SSM_CONV = 4
SSM_CHUNK = 128
NEG_BIG = -1e30
EPS = 1e-6

V7X_VMEM_BYTES = 64 * 1024 * 1024
V7X_LANES = 128
BF16_SUBLANES = 16
VMEM_LIMIT = (V7X_VMEM_BYTES * 7) // 8
LOG2E = math.log2(math.e)
NA_QROWS = 4
NA_BLOCK_UNROLL = 4
CONV_ROW_BLOCKS = 3
DIFF_KEY_CHUNK = 2048
SSD_CHUNK_UNROLL = 4
MOD_SLOTS = 16
CONV_HALO = BF16_SUBLANES


def _params(sem, vmem=VMEM_LIMIT):
    return pltpu.CompilerParams(dimension_semantics=sem, vmem_limit_bytes=vmem)


def _pick(n, want):
    t = min(n, want)
    while n % t:
        t -= 1
    return t


def _silu(a):
    return a * jax.nn.sigmoid(a)


def _modulate(x, g, shift, scale):
    ms = jnp.mean(x * x, axis=-1, keepdims=True)
    return (x * lax.rsqrt(ms + EPS)) * (g * (1.0 + scale)) + shift


def _mod_spec(kind, slot0, rows_per_slot, tm, d, col=False):
    if col:
        return pl.BlockSpec((None, None, 1, d), lambda i, j: (kind, slot0 + (i * tm) // rows_per_slot, 0, j))
    return pl.BlockSpec((None, None, 1, d), lambda i, j: (kind, slot0 + (i * tm) // rows_per_slot, 0, 0))


def _mod_kernel(c_ref, w_ref, b_ref, o_ref):
    a = _silu(c_ref[...])
    o_ref[...] = jnp.dot(a, w_ref[...], preferred_element_type=F32, precision=lax.Precision.HIGHEST) + b_ref[...]


def _modulation(cvec, mod_w, mod_b):
    depth, d, n = mod_w.shape
    tn = _pick(n, 1024)
    return pl.pallas_call(
        _mod_kernel,
        grid=(depth, n // tn),
        in_specs=[
            pl.BlockSpec((MOD_SLOTS, d), lambda l, j: (0, 0)),
            pl.BlockSpec((None, d, tn), lambda l, j: (l, 0, j)),
            pl.BlockSpec((None, 1, tn), lambda l, j: (l, 0, j)),
        ],
        out_specs=pl.BlockSpec((None, MOD_SLOTS, tn), lambda l, j: (l, 0, j)),
        out_shape=jax.ShapeDtypeStruct((depth, MOD_SLOTS, n), F32),
        compiler_params=_params(("parallel", "parallel")),
        name="modulation",
    )(cvec, mod_w, mod_b.reshape(depth, 1, n))


def _ffn_kernel(*refs, final):
    if final:
        x_ref, sh_ref, sc_ref, gt_ref, g_ref, w1_ref, w3_ref, w2_ref, fg_ref, o_ref, h_ref = refs
    else:
        x_ref, sh_ref, sc_ref, gt_ref, g_ref, w1_ref, w3_ref, w2_ref, o_ref, h_ref = refs
    j = pl.program_id(1)

    def chunk(h):
        half = w1_ref.shape[1] // 2
        acc = None
        for c0 in (0, half):
            a = jnp.dot(h, w1_ref[:, c0:c0 + half], preferred_element_type=F32)
            b = jnp.dot(h, w3_ref[:, c0:c0 + half], preferred_element_type=F32)
            t = jnp.dot((_silu(a) * b).astype(BF16), w2_ref[c0:c0 + half, :], preferred_element_type=F32)
            acc = t if acc is None else acc + t
        return acc

    @pl.when(j == 0)
    def _():
        h = _modulate(x_ref[...], g_ref[...], sh_ref[...], sc_ref[...]).astype(BF16)
        h_ref[...] = h
        o_ref[...] = chunk(h)

    @pl.when(j > 0)
    def _():
        o_ref[...] += chunk(h_ref[...])

    @pl.when(j == pl.num_programs(1) - 1)
    def _():
        y = x_ref[...] + (0.5 * gt_ref[...]) * o_ref[...]
        if final:
            ms = jnp.mean(y * y, axis=-1, keepdims=True)
            y = (y * lax.rsqrt(ms + EPS)) * fg_ref[...]
        o_ref[...] = y


def _ffn(x, mod_l, kinds, slot0, rows_per_slot, norm_g, w1, w3, w2, widx, final_g=None):
    m, d = x.shape
    f = w1.shape[-1]
    tm = _pick(rows_per_slot, 512)
    tn = _pick(f, 512)
    nc = f // tn
    final = final_g is not None
    k_sh, k_sc, k_gt = kinds
    la, wh = widx
    in_specs = [
        pl.BlockSpec((tm, d), lambda i, j: (i, 0)),
        _mod_spec(k_sh, slot0, rows_per_slot, tm, d),
        _mod_spec(k_sc, slot0, rows_per_slot, tm, d),
        _mod_spec(k_gt, slot0, rows_per_slot, tm, d),
        pl.BlockSpec((1, d), lambda i, j: (0, 0)),
        pl.BlockSpec((None, None, d, tn), lambda i, j: (la, wh, 0, j)),
        pl.BlockSpec((None, None, d, tn), lambda i, j: (la, wh, 0, j)),
        pl.BlockSpec((None, None, tn, d), lambda i, j: (la, wh, j, 0)),
    ]
    args = [x, mod_l, mod_l, mod_l, norm_g.reshape(1, d), w1, w3, w2]
    if final:
        in_specs.append(pl.BlockSpec((1, d), lambda i, j: (0, 0)))
        args.append(final_g.reshape(1, d))
    return pl.pallas_call(
        functools.partial(_ffn_kernel, final=final),
        grid=(m // tm, nc),
        in_specs=in_specs,
        out_specs=pl.BlockSpec((tm, d), lambda i, j: (i, 0)),
        out_shape=jax.ShapeDtypeStruct((m, d), F32),
        scratch_shapes=[pltpu.VMEM((tm, d), BF16)],
        compiler_params=_params(("parallel", "arbitrary")),
        name="ffn",
    )(*args)


def _rope_tile(res, cos, sin_a, sin_b):
    outs = []
    for k in range(res.shape[1] // HEAD_DIM):
        r = res[:, k * HEAD_DIM:(k + 1) * HEAD_DIM]
        up = pltpu.roll(r, HEAD_DIM - HEAD_DIM // 4, 1)
        dn = pltpu.roll(r, HEAD_DIM // 4, 1)
        outs.append(r * cos + up * sin_a + dn * sin_b)
    return jnp.concatenate(outs, axis=1)


def _proj_kernel(*refs, rope_tiles):
    if rope_tiles:
        x_ref, sh_ref, sc_ref, g_ref, w_ref, cos_ref, sa_ref, sb_ref, o_ref, h_ref = refs
    else:
        x_ref, sh_ref, sc_ref, g_ref, w_ref, o_ref, h_ref = refs
    j = pl.program_id(1)

    @pl.when(j == 0)
    def _():
        h_ref[...] = _modulate(x_ref[...], g_ref[...], sh_ref[...], sc_ref[...]).astype(BF16)

    res = jnp.dot(h_ref[...], w_ref[...], preferred_element_type=F32)
    if rope_tiles:
        is_rope = functools.reduce(jnp.logical_or, [j == t for t in rope_tiles])

        @pl.when(is_rope)
        def _():
            o_ref[...] = _rope_tile(res, cos_ref[...], sa_ref[...], sb_ref[...]).astype(o_ref.dtype)

        @pl.when(jnp.logical_not(is_rope))
        def _():
            o_ref[...] = res.astype(o_ref.dtype)
    else:
        o_ref[...] = res.astype(o_ref.dtype)


def _proj(x, mod_l, kinds, slot0, rows_per_slot, norm_g, w, out_dtype, tn_want, rope=None, cols=None):
    m, d = x.shape
    c0, c1 = cols if cols is not None else (0, w.shape[1])
    n = c1 - c0
    tm = _pick(rows_per_slot, 1024)
    tn = _pick(math.gcd(n, c0) if c0 else n, tn_want)
    cb0 = c0 // tn
    k_sh, k_sc = kinds
    in_specs = [
        pl.BlockSpec((tm, d), lambda i, j: (i, 0)),
        _mod_spec(k_sh, slot0, rows_per_slot, tm, d),
        _mod_spec(k_sc, slot0, rows_per_slot, tm, d),
        pl.BlockSpec((1, d), lambda i, j: (0, 0)),
        pl.BlockSpec((d, tn), lambda i, j: (0, cb0 + j)),
    ]
    args = [x, mod_l, mod_l, norm_g.reshape(1, d), w]
    rope_tiles = ()
    if rope is not None:
        tables, rope_cols = rope
        assert rope_cols[0] % tn == 0 and rope_cols[1] % tn == 0
        rope_tiles = tuple(range(rope_cols[0] // tn, rope_cols[1] // tn))
        tps = rows_per_slot // tm
        for t in tables:
            in_specs.append(pl.BlockSpec((tm, HEAD_DIM), lambda i, j: (i % tps, 0)))
            args.append(t)
    return pl.pallas_call(
        functools.partial(_proj_kernel, rope_tiles=rope_tiles),
        grid=(m // tm, n // tn),
        in_specs=in_specs,
        out_specs=pl.BlockSpec((tm, tn), lambda i, j: (i, j)),
        out_shape=jax.ShapeDtypeStruct((m, n), out_dtype),
        scratch_shapes=[pltpu.VMEM((tm, d), BF16)],
        compiler_params=_params(("parallel", "arbitrary")),
        name="proj",
    )(*args)


def _proj_conv_kernel(xp_ref, x_ref, xn_ref, sh_ref, sc_ref, g_ref, w_ref, cw_ref, cb_ref, o_ref, h_ref,
                      *, tm, n_plain, tiles_per_seq):
    i = pl.program_id(0)
    j = pl.program_id(1)
    n_all = tm + 2 * CONV_HALO

    @pl.when(j == 0)
    def _():
        g, sh, sc = g_ref[...], sh_ref[...], sc_ref[...]
        t = i % tiles_per_seq
        keep_prev = (t != 0).astype(F32)
        keep_next = (t != tiles_per_seq - 1).astype(F32)
        h_ref[0:CONV_HALO, :] = (_modulate(xp_ref[...], g, sh, sc) * keep_prev).astype(BF16)
        h_ref[CONV_HALO:CONV_HALO + tm, :] = _modulate(x_ref[...], g, sh, sc).astype(BF16)
        h_ref[CONV_HALO + tm:n_all, :] = (_modulate(xn_ref[...], g, sh, sc) * keep_next).astype(BF16)

    @pl.when(j < n_plain)
    def _():
        o_ref[...] = jnp.dot(h_ref[CONV_HALO:CONV_HALO + tm, :], w_ref[...],
                             preferred_element_type=F32).astype(o_ref.dtype)

    @pl.when(j >= n_plain)
    def _():
        w = w_ref[...]
        step = n_all // CONV_ROW_BLOCKS if n_all % (CONV_ROW_BLOCKS * BF16_SUBLANES) == 0 else n_all
        u = jnp.concatenate([jnp.dot(h_ref[r0:r0 + step, :], w, preferred_element_type=F32)
                             for r0 in range(0, n_all, step)], axis=0)
        left = (SSM_CONV - 1) // 2
        acc = cb_ref[...] + cw_ref[left:left + 1, :] * u
        for tap in range(SSM_CONV):
            if tap == left:
                continue
            acc = acc + cw_ref[tap:tap + 1, :] * pltpu.roll(u, (left - tap) % n_all, 0)
        y = acc[CONV_HALO:CONV_HALO + tm, :]
        o_ref[...] = _silu(y).astype(o_ref.dtype)


def _proj_conv(x, mod_l, kinds, slot0, rows_per_slot, rows_per_seq, norm_g, w, n_plain_cols, conv_w, conv_b):
    m, d = x.shape
    n = n_plain_cols + conv_w.shape[1]
    tm = _pick(rows_per_seq, 1024)
    tn = _pick(math.gcd(n_plain_cols, n - n_plain_cols), 512)
    n_plain = n_plain_cols // tn
    tps = rows_per_seq // tm
    k_sh, k_sc = kinds
    hb = tm // CONV_HALO
    last_hb = m // CONV_HALO - 1
    in_specs = [
        pl.BlockSpec((CONV_HALO, d), lambda i, j: (jnp.maximum(i * hb - 1, 0), 0)),
        pl.BlockSpec((tm, d), lambda i, j: (i, 0)),
        pl.BlockSpec((CONV_HALO, d), lambda i, j: (jnp.minimum((i + 1) * hb, last_hb), 0)),
        _mod_spec(k_sh, slot0, rows_per_slot, tm, d),
        _mod_spec(k_sc, slot0, rows_per_slot, tm, d),
        pl.BlockSpec((1, d), lambda i, j: (0, 0)),
        pl.BlockSpec((d, tn), lambda i, j: (0, j)),
        pl.BlockSpec((SSM_CONV, tn), lambda i, j: (0, jnp.maximum(j - n_plain, 0))),
        pl.BlockSpec((1, tn), lambda i, j: (0, jnp.maximum(j - n_plain, 0))),
    ]
    return pl.pallas_call(
        functools.partial(_proj_conv_kernel, tm=tm, n_plain=n_plain, tiles_per_seq=tps),
        grid=(m // tm, n // tn),
        in_specs=in_specs,
        out_specs=pl.BlockSpec((tm, tn), lambda i, j: (i, j)),
        out_shape=jax.ShapeDtypeStruct((m, n), BF16),
        scratch_shapes=[pltpu.VMEM((tm + 2 * CONV_HALO, d), BF16)],
        compiler_params=_params(("parallel", "arbitrary")),
        name="proj_conv",
    )(x, x, x, mod_l, mod_l, norm_g.reshape(1, d), w, conv_w, conv_b.reshape(1, -1))


def _outproj_kernel(*refs, n_a):
    x_ref, gt_ref = refs[0], refs[1]
    a_refs = refs[2:2 + n_a]
    w_refs = refs[2 + n_a:2 + 2 * n_a]
    o_ref = refs[2 + 2 * n_a]
    acc = jnp.dot(a_refs[0][...], w_refs[0][...], preferred_element_type=F32)
    for a_ref, w_ref in zip(a_refs[1:], w_refs[1:]):
        acc = acc + jnp.dot(a_ref[...], w_ref[...], preferred_element_type=F32)
    o_ref[...] = x_ref[...] + gt_ref[...] * acc


def _outproj(x, mod_l, k_gt, slot0, rows_per_slot, a_list, w):
    m, d = x.shape
    tm = _pick(rows_per_slot, 1024)
    tn = _pick(d, 512)
    in_specs = [pl.BlockSpec((tm, tn), lambda i, j: (i, j)), _mod_spec(k_gt, slot0, rows_per_slot, tm, tn, col=True)]
    args = [x, mod_l]
    k0 = 0
    w_specs = []
    for a in a_list:
        ka = a.shape[1]
        assert k0 % ka == 0
        in_specs.append(pl.BlockSpec((tm, ka), lambda i, j: (i, 0)))
        w_specs.append(pl.BlockSpec((ka, tn), lambda i, j, kb=k0 // ka: (kb, j)))
        k0 += ka
    assert k0 == w.shape[0]
    return pl.pallas_call(
        functools.partial(_outproj_kernel, n_a=len(a_list)),
        grid=(m // tm, d // tn),
        in_specs=in_specs + w_specs,
        out_specs=pl.BlockSpec((tm, tn), lambda i, j: (i, j)),
        out_shape=jax.ShapeDtypeStruct((m, d), F32),
        compiler_params=_params(("parallel", "parallel")),
        name="outproj",
    )(*args, *a_list, *([w] * len(a_list)))


def _na_plan(rows):
    kr = min(NA_WIN_ROWS, rows)
    qr = NA_QROWS if rows % NA_QROWS == 0 else 1
    slab = min(rows, qr + kr)
    starts, ids, layouts = [], [], []
    for b in range(rows // qr):
        ks = int(np.clip(qr * b - kr // 2, 0, rows - slab))
        lay = []
        for r in range(qr * b, qr * b + qr):
            rs = int(np.clip(r - kr // 2, 0, rows - kr))
            assert ks <= rs and rs + kr <= ks + slab
            lay.append((rs - ks, r - ks))
        lay = tuple(lay)
        if lay not in layouts:
            layouts.append(lay)
        starts.append(ks)
        ids.append(layouts.index(lay))
    return kr, qr, slab, np.array([starts, ids], np.int32), tuple(layouts)


def _na_bias_kernel(rpb_ref, o_ref, *, kr, slab, layouts):
    h = pl.program_id(0)
    n_ro, n_co = 2 * NA_WIN_ROWS - 1, 2 * NA_WIN_COLS - 1
    qc = lax.broadcasted_iota(jnp.int32, (GRID_W, GRID_W), 0)
    kc = lax.broadcasted_iota(jnp.int32, (GRID_W, GRID_W), 1)
    dcol = kc - qc + (NA_WIN_COLS - 1)
    cs = jnp.clip(qc - NA_WIN_COLS // 2, 0, GRID_W - NA_WIN_COLS)
    valid = jnp.logical_and(kc >= cs, kc < cs + NA_WIN_COLS)
    tabs = []
    for ro in range(n_ro):
        acc = jnp.zeros((GRID_W, GRID_W), F32)
        for dd in range(n_co):
            acc = jnp.where(dcol == dd, rpb_ref[(h * n_ro + ro) * n_co + dd], acc)
        tabs.append(jnp.where(valid, acc * LOG2E, NEG_BIG))
    masked = jnp.full((GRID_W, GRID_W), NEG_BIG, F32)
    for ti, lay in enumerate(layouts):
        for qi, (win0, qrow) in enumerate(lay):
            for ki in range(slab):
                ro = ki - qrow + (NA_WIN_ROWS - 1)
                tile = tabs[ro] if win0 <= ki < win0 + kr else masked
                o_ref[ti, qi * GRID_W:(qi + 1) * GRID_W, ki * GRID_W:(ki + 1) * GRID_W] = tile


def _na_bias(rpb, plan):
    kr, qr, slab, _, layouts = plan
    nh = rpb.shape[0]
    shape = (len(layouts), qr * GRID_W, slab * GRID_W)
    return pl.pallas_call(
        functools.partial(_na_bias_kernel, kr=kr, slab=slab, layouts=layouts),
        grid=(nh,),
        in_specs=[pl.BlockSpec(memory_space=pltpu.SMEM)],
        out_specs=pl.BlockSpec((None,) + shape, lambda h: (h, 0, 0, 0)),
        out_shape=jax.ShapeDtypeStruct((nh,) + shape, F32),
        compiler_params=_params(("parallel",)),
        name="na_bias",
    )(rpb.reshape(-1))


def _dot_t(a, b):
    return lax.dot_general(a, b, (((1,), (1,)), ((), ())), preferred_element_type=F32)


def _na_kernel(plan_ref, q_ref, k_ref, v_ref, kc_ref, vc_ref, bias_ref, o_ref, *, n_blocks, nq, nk):
    c = HEAD_DIM ** -0.5 * LOG2E
    kc = kc_ref[...]
    vc = vc_ref[...]

    def block(blk, carry):
        q0 = pl.multiple_of(blk * nq, nq)
        k0 = pl.multiple_of(plan_ref[0, blk] * GRID_W, GRID_W)
        q = q_ref[pl.ds(q0, nq), :]
        s = _dot_t(q, k_ref[pl.ds(k0, nk), :]) * c + bias_ref[plan_ref[1, blk]]
        sc = _dot_t(q, kc) * c
        m = jnp.maximum(jnp.max(s, axis=-1, keepdims=True), jnp.max(sc, axis=-1, keepdims=True))
        p = jnp.exp2(s - m)
        pc = jnp.exp2(sc - m)
        inv = 1.0 / (jnp.sum(p, axis=-1, keepdims=True) + jnp.sum(pc, axis=-1, keepdims=True))
        o = (jnp.dot(p.astype(BF16), v_ref[pl.ds(k0, nk), :], preferred_element_type=F32)
             + jnp.dot(pc.astype(BF16), vc, preferred_element_type=F32)) * inv
        o_ref[pl.ds(q0, nq), :] = o.astype(o_ref.dtype)
        return carry

    lax.fori_loop(0, n_blocks, block, 0, unroll=NA_BLOCK_UNROLL)


def _na_attention(qkv_x, qkv_c, bias, plan, batch, seq, ctx_len, n_heads):
    _, qr, slab, plan_arr, layouts = plan
    tab = (len(layouts), qr * GRID_W, slab * GRID_W)
    return pl.pallas_call(
        functools.partial(_na_kernel, n_blocks=plan_arr.shape[1], nq=qr * GRID_W, nk=slab * GRID_W),
        grid=(n_heads, batch),
        in_specs=[
            pl.BlockSpec(memory_space=pltpu.SMEM),
            pl.BlockSpec((seq, HEAD_DIM), lambda h, b: (b, h)),
            pl.BlockSpec((seq, HEAD_DIM), lambda h, b: (b, n_heads + h)),
            pl.BlockSpec((seq, HEAD_DIM), lambda h, b: (b, 2 * n_heads + h)),
            pl.BlockSpec((ctx_len, HEAD_DIM), lambda h, b: (b, n_heads + h)),
            pl.BlockSpec((ctx_len, HEAD_DIM), lambda h, b: (b, 2 * n_heads + h)),
            pl.BlockSpec((None,) + tab, lambda h, b: (h, 0, 0, 0)),
        ],
        out_specs=pl.BlockSpec((seq, HEAD_DIM), lambda h, b: (b, h)),
        out_shape=jax.ShapeDtypeStruct((batch * seq, n_heads * HEAD_DIM), BF16),
        compiler_params=_params(("parallel", "parallel")),
        name="na_attn",
    )(jnp.asarray(plan_arr), qkv_x, qkv_x, qkv_x, qkv_c, qkv_c, bias)


def _ctx_attn_kernel(q_ref, k_ref, v_ref, o_ref):
    s = _dot_t(q_ref[...], k_ref[...]) * HEAD_DIM ** -0.5
    p = jnp.exp(s - jnp.max(s, axis=-1, keepdims=True))
    p = p * (1.0 / jnp.sum(p, axis=-1, keepdims=True))
    o_ref[...] = jnp.dot(p.astype(BF16), v_ref[...], preferred_element_type=F32).astype(o_ref.dtype)


def _ctx_attention(qkv_c, batch, ctx_len, n_heads):
    return pl.pallas_call(
        _ctx_attn_kernel,
        grid=(batch, n_heads),
        in_specs=[
            pl.BlockSpec((ctx_len, HEAD_DIM), lambda b, h: (b, h)),
            pl.BlockSpec((ctx_len, HEAD_DIM), lambda b, h: (b, n_heads + h)),
            pl.BlockSpec((ctx_len, HEAD_DIM), lambda b, h: (b, 2 * n_heads + h)),
        ],
        out_specs=pl.BlockSpec((ctx_len, HEAD_DIM), lambda b, h: (b, h)),
        out_shape=jax.ShapeDtypeStruct((batch * ctx_len, n_heads * HEAD_DIM), BF16),
        compiler_params=_params(("parallel", "parallel")),
        name="ctx_attn",
    )(qkv_c, qkv_c, qkv_c)


def _diff_kernel(*refs, n_src, lam_init, src_rows):
    lam_ref, g_ref, q_ref = refs[0], refs[1], refs[2]
    k_refs = refs[3:3 + n_src]
    v_refs = refs[3 + n_src:3 + 2 * n_src]
    o_ref = refs[3 + 2 * n_src]
    acc_ref = refs[4 + 2 * n_src]
    tq = q_ref.shape[0]
    c = HEAD_DIM ** -0.5 * LOG2E
    lf = lam_ref[...]
    lam = (jnp.exp(jnp.sum(lf[0:1] * lf[1:2], axis=-1, keepdims=True))
           - jnp.exp(jnp.sum(lf[2:3] * lf[3:4], axis=-1, keepdims=True)) + lam_init)
    qs = [q_ref[:, mi * HEAD_DIM:(mi + 1) * HEAD_DIM] for mi in range(2)]
    acc_ref[...] = jnp.zeros_like(acc_ref)

    def chunk(carry, k_ref, v_ref, r0, n):
        rows = pl.ds(r0, n)
        v = v_ref[rows, :]
        out = []
        for mi in range(2):
            m, l = carry[2 * mi], carry[2 * mi + 1]
            s = _dot_t(qs[mi], k_ref[rows, mi * HEAD_DIM:(mi + 1) * HEAD_DIM])
            m_new = jnp.maximum(m, jnp.max(s, axis=-1, keepdims=True))
            a = jnp.exp2((m - m_new) * c)
            p = jnp.exp2((s - m_new) * c)
            acc_ref[mi] = acc_ref[mi] * a + jnp.dot(p.astype(BF16), v, preferred_element_type=F32)
            out += [m_new, l * a + jnp.sum(p, axis=-1, keepdims=True)]
        return tuple(out)

    carry = (jnp.full((tq, 1), NEG_BIG, F32), jnp.zeros((tq, 1), F32)) * 2
    for k_ref, v_ref, rows in zip(k_refs, v_refs, src_rows):
        ck = _pick(rows, DIFF_KEY_CHUNK)
        if rows == ck:
            carry = chunk(carry, k_ref, v_ref, 0, ck)
        else:
            carry = lax.fori_loop(
                0, rows // ck,
                lambda i, cr, k_ref=k_ref, v_ref=v_ref, ck=ck: chunk(cr, k_ref, v_ref, pl.multiple_of(i * ck, ck), ck),
                carry, unroll=True)
    _, l1, _, l2 = carry
    o = acc_ref[0] * (1.0 / l1) - acc_ref[1] * (lam / l2)
    ms = jnp.mean(o * o, axis=-1, keepdims=True)
    o_ref[...] = ((o * lax.rsqrt(ms + EPS)) * g_ref[...] * (1.0 - lam_init)).astype(o_ref.dtype)


def _diff_attention(q_arr, q_rows, kv_list, lam_vecs, subln_g, lam_init, batch, n_heads, q_col0, tq_want):
    hw = 2 * HEAD_DIM
    tq = _pick(q_rows, tq_want)
    nq = q_rows // tq
    qb = q_col0 // hw
    in_specs = [
        pl.BlockSpec((4, HEAD_DIM), lambda b, h, i: (0, 0)),
        pl.BlockSpec((1, hw), lambda b, h, i: (0, 0)),
        pl.BlockSpec((tq, hw), lambda b, h, i: (b * nq + i, qb + h)),
    ]
    args = [lam_vecs, subln_g.reshape(1, hw), q_arr]
    for arr, rows in kv_list:
        in_specs.append(pl.BlockSpec((rows, hw), lambda b, h, i: (b, qb + n_heads + h)))
        args.append(arr)
    for arr, rows in kv_list:
        in_specs.append(pl.BlockSpec((rows, hw), lambda b, h, i: (b, qb + 2 * n_heads + h)))
        args.append(arr)
    return pl.pallas_call(
        functools.partial(_diff_kernel, n_src=len(kv_list), lam_init=lam_init,
                          src_rows=tuple(rows for _, rows in kv_list)),
        grid=(batch, n_heads, nq),
        in_specs=in_specs,
        out_specs=pl.BlockSpec((tq, hw), lambda b, h, i: (b * nq + i, h)),
        out_shape=jax.ShapeDtypeStruct((batch * q_rows, n_heads * hw), BF16),
        scratch_shapes=[pltpu.VMEM((2, tq, hw), F32)],
        compiler_params=_params(("parallel", "parallel", "parallel")),
        name="diff_attn",
    )(*args)


def _cumsum_rows(a):
    n = a.shape[0]
    row = lax.broadcasted_iota(jnp.int32, a.shape, 0)
    d = 1
    while d < n:
        a = a + jnp.where(row >= d, pltpu.roll(a, d, 0), 0.0)
        d *= 2
    return a


def _ssd_kernel(*refs, n_ctx_chunks, n_x_chunks, hpg, n_heads, ctx_out):
    (xs_x, z_x, b_x, c_x, dt_x, xs_c, z_c, b_c, c_c, dt_c, bias_ref, alog_ref, dsum_ref, ng_ref) = refs[:14]
    n_out = 2 if ctx_out else 1
    yx_ref = refs[14]
    yc_ref = refs[15] if ctx_out else None
    yacc_x, yacc_c, sc_x, sc_c, rft_x, rft_c, hf_ref, hb_ref = refs[14 + n_out:]
    t = SSM_CHUNK
    pw = 2 * SSM_HEADDIM
    g = pl.program_id(1)
    nl = 2 * n_heads

    lane = lax.broadcasted_iota(jnp.int32, (t, nl), 1)
    is_fwd = lane < n_heads
    bias = bias_ref[...]
    neg_a2 = -jnp.exp(alog_ref[...]) * LOG2E
    li = lax.broadcasted_iota(jnp.int32, (t, t), 0)
    si = lax.broadcasted_iota(jnp.int32, (t, t), 1)
    add_lo = jnp.where(li >= si, 0.0, NEG_BIG)
    add_up = jnp.where(li <= si, 0.0, NEG_BIG)
    pair_lane = lax.broadcasted_iota(jnp.int32, (t, pw), 1) < SSM_HEADDIM
    lane_shift = (nl - g * hpg) % nl

    def pair_weights(xp):
        zero = jnp.zeros_like(xp)
        return jnp.concatenate([jnp.where(pair_lane, xp, zero), jnp.where(pair_lane, zero, xp)], axis=0)

    def col(scl, idx):
        return jnp.broadcast_to(scl[:, idx:idx + 1], (t, t))

    def fwd_chunk(xs_ref, b_ref, c_ref, dt_ref, yacc, sc_ref, rft_ref, r0, h0):
        rows = pl.ds(r0, t)
        dt = jax.nn.softplus(dt_ref[rows, :] + bias)
        a2 = dt * neg_a2
        cs = _cumsum_rows(a2)
        sc2 = jnp.where(is_fwd, cs, cs[t - 1:t, :] - cs + a2)
        sc_ref[rows, :] = sc2
        rft_ref[pl.ds(h0, nl), :] = (sc2 - jnp.log(dt) * LOG2E).T
        scl = pltpu.roll(sc2, lane_shift, 1)
        xb = xs_ref[rows, :]
        bb = b_ref[rows, :]
        cb_ = c_ref[rows, :]
        cbm = _dot_t(cb_, bb)
        bt = bb.astype(F32).T
        yoff = jnp.dot(cb_, hf_ref[...].astype(BF16), preferred_element_type=F32)
        for jp in range(hpg // 2):
            ms, bws, es = [], [], []
            for j in (2 * jp, 2 * jp + 1):
                cf = col(scl, j)
                cbk = col(scl, n_heads + j)
                rf = rft_ref[pl.ds(h0 + g * hpg + j, 1), :]
                rb = rft_ref[pl.ds(h0 + n_heads + g * hpg + j, 1), :]
                ms.append((cbm * (jnp.exp2(cf - rf + add_lo) + jnp.exp2(cbk - rb + add_up))).astype(BF16))
                bws.append((bt * jnp.exp2(cf[t - 1:t, :] - rf)).astype(BF16))
                es.append(jnp.exp2(cf))
            cols = slice(jp * pw, (jp + 1) * pw)
            xp = xb[:, cols]
            lhs = jnp.concatenate([jnp.concatenate(ms, axis=1), jnp.concatenate(bws, axis=1)], axis=0)
            res = jnp.dot(lhs, pair_weights(xp), preferred_element_type=F32)
            e_pair = jnp.where(pair_lane, es[0], es[1])
            yacc[rows, cols] = res[:t] + xp.astype(F32) * dsum_ref[:, cols] + yoff[:, cols] * e_pair
            hf_ref[:, cols] = hf_ref[:, cols] * e_pair[t - 1:t, :] + res[t:]

    def bwd_chunk(xs_ref, z_ref, b_ref, c_ref, yacc, sc_ref, rft_ref, y_ref, r0, h0):
        rows = pl.ds(r0, t)
        scl = pltpu.roll(sc_ref[rows, :], lane_shift, 1)
        xb = xs_ref[rows, :]
        bt = b_ref[rows, :].astype(F32).T
        yoff = jnp.dot(c_ref[rows, :], hb_ref[...].astype(BF16), preferred_element_type=F32)
        ys = []
        for jp in range(hpg // 2):
            bws, es = [], []
            for j in (2 * jp, 2 * jp + 1):
                cbk = col(scl, n_heads + j)
                rb = rft_ref[pl.ds(h0 + n_heads + g * hpg + j, 1), :]
                bws.append((bt * jnp.exp2(cbk[0:1, :] - rb)).astype(BF16))
                es.append(jnp.exp2(cbk))
            cols = slice(jp * pw, (jp + 1) * pw)
            grow = jnp.dot(jnp.concatenate(bws, axis=1), pair_weights(xb[:, cols]), preferred_element_type=F32)
            e_pair = jnp.where(pair_lane, es[0], es[1])
            ys.append(yacc[rows, cols] + yoff[:, cols] * e_pair)
            hb_ref[:, cols] = hb_ref[:, cols] * e_pair[0:1, :] + grow
        if y_ref is not None:
            yg = jnp.concatenate(ys, axis=1) * _silu(z_ref[rows, :].astype(F32))
            ms = jnp.mean(yg * yg, axis=-1, keepdims=True)
            y_ref[rows, :] = ((yg * lax.rsqrt(ms + EPS)) * ng_ref[...]).astype(y_ref.dtype)

    hf_ref[...] = jnp.zeros_like(hf_ref)
    hb_ref[...] = jnp.zeros_like(hb_ref)
    for k in range(n_ctx_chunks):
        fwd_chunk(xs_c, b_c, c_c, dt_c, yacc_c, sc_c, rft_c, k * t, k * nl)

    def fx(k, carry):
        fwd_chunk(xs_x, b_x, c_x, dt_x, yacc_x, sc_x, rft_x, pl.multiple_of(k * t, t), pl.multiple_of(k * nl, nl))
        return carry

    lax.fori_loop(0, n_x_chunks, fx, 0, unroll=SSD_CHUNK_UNROLL)
    for k in reversed(range(n_ctx_chunks)):
        bwd_chunk(xs_c, z_c, b_c, c_c, yacc_c, sc_c, rft_c, yc_ref, k * t, k * nl)

    def bx(k, carry):
        kk = n_x_chunks - 1 - k
        bwd_chunk(xs_x, z_x, b_x, c_x, yacc_x, sc_x, rft_x, yx_ref, pl.multiple_of(kk * t, t),
                  pl.multiple_of(kk * nl, nl))
        return carry

    lax.fori_loop(0, n_x_chunks, bx, 0, unroll=SSD_CHUNK_UNROLL)


def _ssd(zx, zc, dtx, dtc, a_log, dt_bias, d_skip, norm_g, batch, seq, ctx_len, ctx_out):
    inner = norm_g.shape[0]
    n_heads = inner // SSM_HEADDIM
    hpg = n_heads // SSM_GROUPS
    gw = hpg * SSM_HEADDIM
    nl = 2 * n_heads
    assert hpg % 2 == 0 and seq % SSM_CHUNK == 0 and ctx_len % SSM_CHUNK == 0
    assert SSM_CHUNK == 2 * SSM_HEADDIM == SSM_STATE == V7X_LANES and nl == V7X_LANES
    zb = inner // gw
    b0 = (2 * inner) // SSM_STATE
    c0 = b0 + SSM_GROUPS

    def specs(rows):
        return [
            pl.BlockSpec((rows, gw), lambda b, g: (b, zb + g)),
            pl.BlockSpec((rows, gw), lambda b, g: (b, g)),
            pl.BlockSpec((rows, SSM_STATE), lambda b, g: (b, b0 + g)),
            pl.BlockSpec((rows, SSM_STATE), lambda b, g: (b, c0 + g)),
            pl.BlockSpec((rows, nl), lambda b, g: (b, 0)),
        ]

    dsum = jnp.repeat(d_skip[0] + d_skip[1], SSM_HEADDIM).reshape(1, inner).astype(F32)
    in_specs = specs(seq) + specs(ctx_len) + [
        pl.BlockSpec((1, nl), lambda b, g: (0, 0)),
        pl.BlockSpec((1, nl), lambda b, g: (0, 0)),
        pl.BlockSpec((1, gw), lambda b, g: (0, g)),
        pl.BlockSpec((1, gw), lambda b, g: (0, g)),
    ]
    out_specs = [pl.BlockSpec((seq, gw), lambda b, g: (b, g))]
    out_shape = [jax.ShapeDtypeStruct((batch * seq, inner), BF16)]
    if ctx_out:
        out_specs.append(pl.BlockSpec((ctx_len, gw), lambda b, g: (b, g)))
        out_shape.append(jax.ShapeDtypeStruct((batch * ctx_len, inner), BF16))
    ncx, ncc = seq // SSM_CHUNK, ctx_len // SSM_CHUNK
    scratch = [
        pltpu.VMEM((seq, gw), F32), pltpu.VMEM((ctx_len, gw), F32),
        pltpu.VMEM((seq, nl), F32), pltpu.VMEM((ctx_len, nl), F32),
        pltpu.VMEM((ncx * nl, SSM_CHUNK), F32), pltpu.VMEM((ncc * nl, SSM_CHUNK), F32),
        pltpu.VMEM((SSM_STATE, gw), F32), pltpu.VMEM((SSM_STATE, gw), F32),
    ]
    outs = pl.pallas_call(
        functools.partial(_ssd_kernel, n_ctx_chunks=ctx_len // SSM_CHUNK, n_x_chunks=seq // SSM_CHUNK,
                          hpg=hpg, n_heads=n_heads, ctx_out=ctx_out),
        grid=(batch, SSM_GROUPS),
        in_specs=in_specs,
        out_specs=out_specs,
        out_shape=out_shape,
        scratch_shapes=scratch,
        compiler_params=_params(("parallel", "parallel")),
        name="ssd",
    )(zx, zx, zx, zx, dtx, zc, zc, zc, zc, dtc,
      dt_bias.reshape(1, nl), a_log.reshape(1, nl), dsum, norm_g.reshape(1, inner))
    return (outs[0], outs[1]) if ctx_out else (outs[0], None)


def _rope_tables(seq):
    quarter = HEAD_DIM // 4
    inv = 1.0 / (ROPE_THETA ** (jnp.arange(quarter, dtype=F32) / quarter))
    t = jnp.arange(seq)
    row = (t // GRID_W).astype(F32)[:, None] * inv
    col = (t % GRID_W).astype(F32)[:, None] * inv
    ang = jnp.concatenate([row, row, col, col], axis=-1)
    cos, sin = jnp.cos(ang), jnp.sin(ang)
    first = (np.arange(HEAD_DIM) % (2 * quarter)) < quarter
    return cos, jnp.where(first, -sin, 0.0), jnp.where(first, 0.0, sin)


def kernel(x, c, ctx, c_ctx, mod_w, mod_b, norm_g, ffn_w1, ffn_w3, ffn_w2, attn_w_in, attn_w_out, na_rpb,
           diff_lambda, diff_subln_g, ssm_w_in, ssm_conv_w, ssm_conv_b, ssm_a_log, ssm_dt_bias, ssm_d,
           ssm_norm_g, ssm_w_out, final_norm_g):
    batch, seq, d = x.shape
    ctx_len = ctx.shape[1]
    depth = mod_w.shape[0]
    assert batch + 1 <= MOD_SLOTS
    na_w = d // 2
    na_heads = na_w // HEAD_DIM
    diff_heads = na_w // (2 * HEAD_DIM)
    inner = ssm_norm_g.shape[1]
    n_ssm_heads = inner // SSM_HEADDIM

    cvec = jnp.zeros((MOD_SLOTS, d), F32).at[:batch].set(c).at[batch].set(c_ctx)
    mod = _modulation(cvec, mod_w, mod_b)
    mod = mod.reshape(depth, MOD_SLOTS, N_MOD, d).transpose(0, 2, 1, 3)[:, :, :, None, :]

    xs = x.reshape(batch * seq, d)
    hs = ctx.reshape(batch * ctx_len, d)
    rope = _rope_tables(seq)
    na_plan = _na_plan(seq // GRID_W)
    xslot = (0, seq)
    cslot = (batch, batch * ctx_len)
    ffn_w = tuple(w.astype(BF16) for w in (ffn_w1, ffn_w3, ffn_w2))

    for layer in range(depth):
        ctx_out = layer < depth - 1
        last = layer == depth - 1
        ml = mod[layer]
        ng = norm_g[layer]
        xs = _ffn(xs, ml, (0, 1, 2), *xslot, ng[0], *ffn_w, (layer, 0))
        hs = _ffn(hs, ml, (0, 1, 2), *cslot, ng[0], *ffn_w, (layer, 0))

        i = layer // 2
        if layer % 2 == 0:
            lam_init = 0.8 - 0.6 * math.exp(-0.3 * layer)
            w_in = attn_w_in[i].astype(BF16)
            w_out = attn_w_out[i].astype(BF16)
            dq0 = 3 * na_w
            qkv_x = _proj(xs, ml, (3, 4), *xslot, ng[1], w_in, BF16, 1024, rope=(rope, (dq0, dq0 + 2 * na_w)))
            qkv_c = _proj(hs, ml, (3, 4), *cslot, ng[1], w_in, BF16, 1024)
            bias = _na_bias(na_rpb[i], na_plan)
            na_x = _na_attention(qkv_x, qkv_c, bias, na_plan, batch, seq, ctx_len, na_heads)
            df_x = _diff_attention(qkv_x, seq, [(qkv_x, seq), (qkv_c, ctx_len)], diff_lambda[i], diff_subln_g[i],
                                   lam_init, batch, diff_heads, dq0, 512)
            xs = _outproj(xs, ml, 5, *xslot, [na_x, df_x], w_out)
            if ctx_out:
                na_c = _ctx_attention(qkv_c, batch, ctx_len, na_heads)
                df_c = _diff_attention(qkv_c, ctx_len, [(qkv_c, ctx_len)], diff_lambda[i], diff_subln_g[i],
                                       lam_init, batch, diff_heads, dq0, 256)
                hs = _outproj(hs, ml, 5, *cslot, [na_c, df_c], w_out)
        else:
            n_zx = 2 * inner + 2 * SSM_GROUPS * SSM_STATE
            w_in = ssm_w_in[i].astype(BF16)
            dt_cols = (n_zx, w_in.shape[1])
            cw, cb = ssm_conv_w[i], ssm_conv_b[i]
            zx = _proj_conv(xs, ml, (3, 4), *xslot, seq, ng[1], w_in, inner, cw, cb)
            zc = _proj_conv(hs, ml, (3, 4), *cslot, ctx_len, ng[1], w_in, inner, cw, cb)
            dtx = _proj(xs, ml, (3, 4), *xslot, ng[1], w_in, F32, 2 * n_ssm_heads, cols=dt_cols)
            dtc = _proj(hs, ml, (3, 4), *cslot, ng[1], w_in, F32, 2 * n_ssm_heads, cols=dt_cols)
            yx, yc = _ssd(zx, zc, dtx, dtc, ssm_a_log[i], ssm_dt_bias[i], ssm_d[i], ssm_norm_g[i],
                          batch, seq, ctx_len, ctx_out)
            w_out = ssm_w_out[i].astype(BF16)
            xs = _outproj(xs, ml, 5, *xslot, [yx], w_out)
            if ctx_out:
                hs = _outproj(hs, ml, 5, *cslot, [yc], w_out)

        xs = _ffn(xs, ml, (6, 7, 8), *xslot, ng[2], *ffn_w, (layer, 1), final_g=final_norm_g if last else None)
        if ctx_out:
            hs = _ffn(hs, ml, (6, 7, 8), *cslot, ng[2], *ffn_w, (layer, 1))
    return xs.reshape(batch, seq, d)
```

```python
import functools
import math

import jax
import jax.numpy as jnp
import numpy as np
from jax import lax
from jax.experimental import pallas as pl
from jax.experimental.pallas import tpu as pltpu

F32 = jnp.float32
BF16 = jnp.bfloat16

HEAD_DIM = 128
GRID_W = 64
N_MOD = 9
NA_WIN_ROWS = 8
NA_WIN_COLS = 16
ROPE_THETA = 10000.0
SSM_HEADDIM = 64
SSM_STATE = 128
SSM_GROUPS = 8
SSM_CONV = 4
SSM_CHUNK = 128
NEG_BIG = -1e30
EPS = 1e-6

V7X_VMEM_BYTES = 64 * 1024 * 1024
V7X_LANES = 128
BF16_SUBLANES = 16
VMEM_LIMIT = (V7X_VMEM_BYTES * 7) // 8
LOG2E = math.log2(math.e)
NA_QROWS = 4
NA_BLOCK_UNROLL = 4
CONV_ROW_BLOCKS = 3
DIFF_KEY_CHUNK = 2048
SSD_CHUNK_UNROLL = 4
MOD_SLOTS = 16
CONV_HALO = BF16_SUBLANES


def _params(sem, vmem=VMEM_LIMIT):
    return pltpu.CompilerParams(dimension_semantics=sem, vmem_limit_bytes=vmem)


def _pick(n, want):
    t = min(n, want)
    while n % t:
        t -= 1
    return t


def _silu(a):
    return a * jax.nn.sigmoid(a)


def _modulate(x, g, shift, scale):
    ms = jnp.mean(x * x, axis=-1, keepdims=True)
    return (x * lax.rsqrt(ms + EPS)) * (g * (1.0 + scale)) + shift


def _mod_spec(kind, slot0, rows_per_slot, tm, d, col=False):
    if col:
        return pl.BlockSpec((None, None, 1, d), lambda i, j: (kind, slot0 + (i * tm) // rows_per_slot, 0, j))
    return pl.BlockSpec((None, None, 1, d), lambda i, j: (kind, slot0 + (i * tm) // rows_per_slot, 0, 0))


def _mod_kernel(c_ref, w_ref, b_ref, o_ref):
    a = _silu(c_ref[...])
    o_ref[...] = jnp.dot(a, w_ref[...], preferred_element_type=F32, precision=lax.Precision.HIGHEST) + b_ref[...]


def _modulation(cvec, mod_w, mod_b):
    depth, d, n = mod_w.shape
    tn = _pick(n, 1024)
    return pl.pallas_call(
        _mod_kernel,
        grid=(depth, n // tn),
        in_specs=[
            pl.BlockSpec((MOD_SLOTS, d), lambda l, j: (0, 0)),
            pl.BlockSpec((None, d, tn), lambda l, j: (l, 0, j)),
            pl.BlockSpec((None, 1, tn), lambda l, j: (l, 0, j)),
        ],
        out_specs=pl.BlockSpec((None, MOD_SLOTS, tn), lambda l, j: (l, 0, j)),
        out_shape=jax.ShapeDtypeStruct((depth, MOD_SLOTS, n), F32),
        compiler_params=_params(("parallel", "parallel")),
        name="modulation",
    )(cvec, mod_w, mod_b.reshape(depth, 1, n))


def _ffn_kernel(*refs, final):
    if final:
        x_ref, sh_ref, sc_ref, gt_ref, g_ref, w1_ref, w3_ref, w2_ref, fg_ref, o_ref, h_ref = refs
    else:
        x_ref, sh_ref, sc_ref, gt_ref, g_ref, w1_ref, w3_ref, w2_ref, o_ref, h_ref = refs
    j = pl.program_id(1)

    def chunk(h):
        half = w1_ref.shape[1] // 2
        acc = None
        for c0 in (0, half):
            a = jnp.dot(h, w1_ref[:, c0:c0 + half], preferred_element_type=F32)
            b = jnp.dot(h, w3_ref[:, c0:c0 + half], preferred_element_type=F32)
            t = jnp.dot((_silu(a) * b).astype(BF16), w2_ref[c0:c0 + half, :], preferred_element_type=F32)
            acc = t if acc is None else acc + t
        return acc

    @pl.when(j == 0)
    def _():
        h = _modulate(x_ref[...], g_ref[...], sh_ref[...], sc_ref[...]).astype(BF16)
        h_ref[...] = h
        o_ref[...] = chunk(h)

    @pl.when(j > 0)
    def _():
        o_ref[...] += chunk(h_ref[...])

    @pl.when(j == pl.num_programs(1) - 1)
    def _():
        y = x_ref[...] + (0.5 * gt_ref[...]) * o_ref[...]
        if final:
            ms = jnp.mean(y * y, axis=-1, keepdims=True)
            y = (y * lax.rsqrt(ms + EPS)) * fg_ref[...]
        o_ref[...] = y


def _ffn(x, mod_l, kinds, slot0, rows_per_slot, norm_g, w1, w3, w2, widx, final_g=None):
    m, d = x.shape
    f = w1.shape[-1]
    tm = _pick(rows_per_slot, 512)
    tn = _pick(f, 512)
    nc = f // tn
    final = final_g is not None
    k_sh, k_sc, k_gt = kinds
    la, wh = widx
    in_specs = [
        pl.BlockSpec((tm, d), lambda i, j: (i, 0)),
        _mod_spec(k_sh, slot0, rows_per_slot, tm, d),
        _mod_spec(k_sc, slot0, rows_per_slot, tm, d),
        _mod_spec(k_gt, slot0, rows_per_slot, tm, d),
        pl.BlockSpec((1, d), lambda i, j: (0, 0)),
        pl.BlockSpec((None, None, d, tn), lambda i, j: (la, wh, 0, j)),
        pl.BlockSpec((None, None, d, tn), lambda i, j: (la, wh, 0, j)),
        pl.BlockSpec((None, None, tn, d), lambda i, j: (la, wh, j, 0)),
    ]
    args = [x, mod_l, mod_l, mod_l, norm_g.reshape(1, d), w1, w3, w2]
    if final:
        in_specs.append(pl.BlockSpec((1, d), lambda i, j: (0, 0)))
        args.append(final_g.reshape(1, d))
    return pl.pallas_call(
        functools.partial(_ffn_kernel, final=final),
        grid=(m // tm, nc),
        in_specs=in_specs,
        out_specs=pl.BlockSpec((tm, d), lambda i, j: (i, 0)),
        out_shape=jax.ShapeDtypeStruct((m, d), F32),
        scratch_shapes=[pltpu.VMEM((tm, d), BF16)],
        compiler_params=_params(("parallel", "arbitrary")),
        name="ffn",
    )(*args)


def _rope_tile(res, cos, sin_a, sin_b):
    outs = []
    for k in range(res.shape[1] // HEAD_DIM):
        r = res[:, k * HEAD_DIM:(k + 1) * HEAD_DIM]
        up = pltpu.roll(r, HEAD_DIM - HEAD_DIM // 4, 1)
        dn = pltpu.roll(r, HEAD_DIM // 4, 1)
        outs.append(r * cos + up * sin_a + dn * sin_b)
    return jnp.concatenate(outs, axis=1)


def _proj_kernel(*refs, rope_tiles):
    if rope_tiles:
        x_ref, sh_ref, sc_ref, g_ref, w_ref, cos_ref, sa_ref, sb_ref, o_ref, h_ref = refs
    else:
        x_ref, sh_ref, sc_ref, g_ref, w_ref, o_ref, h_ref = refs
    j = pl.program_id(1)

    @pl.when(j == 0)
    def _():
        h_ref[...] = _modulate(x_ref[...], g_ref[...], sh_ref[...], sc_ref[...]).astype(BF16)

    res = jnp.dot(h_ref[...], w_ref[...], preferred_element_type=F32)
    if rope_tiles:
        is_rope = functools.reduce(jnp.logical_or, [j == t for t in rope_tiles])

        @pl.when(is_rope)
        def _():
            o_ref[...] = _rope_tile(res, cos_ref[...], sa_ref[...], sb_ref[...]).astype(o_ref.dtype)

        @pl.when(jnp.logical_not(is_rope))
        def _():
            o_ref[...] = res.astype(o_ref.dtype)
    else:
        o_ref[...] = res.astype(o_ref.dtype)


def _proj(x, mod_l, kinds, slot0, rows_per_slot, norm_g, w, out_dtype, tn_want, rope=None, cols=None):
    m, d = x.shape
    c0, c1 = cols if cols is not None else (0, w.shape[1])
    n = c1 - c0
    tm = _pick(rows_per_slot, 1024)
    tn = _pick(math.gcd(n, c0) if c0 else n, tn_want)
    cb0 = c0 // tn
    k_sh, k_sc = kinds
    in_specs = [
        pl.BlockSpec((tm, d), lambda i, j: (i, 0)),
        _mod_spec(k_sh, slot0, rows_per_slot, tm, d),
        _mod_spec(k_sc, slot0, rows_per_slot, tm, d),
        pl.BlockSpec((1, d), lambda i, j: (0, 0)),
        pl.BlockSpec((d, tn), lambda i, j: (0, cb0 + j)),
    ]
    args = [x, mod_l, mod_l, norm_g.reshape(1, d), w]
    rope_tiles = ()
    if rope is not None:
        tables, rope_cols = rope
        assert rope_cols[0] % tn == 0 and rope_cols[1] % tn == 0
        rope_tiles = tuple(range(rope_cols[0] // tn, rope_cols[1] // tn))
        tps = rows_per_slot // tm
        for t in tables:
            in_specs.append(pl.BlockSpec((tm, HEAD_DIM), lambda i, j: (i % tps, 0)))
            args.append(t)
    return pl.pallas_call(
        functools.partial(_proj_kernel, rope_tiles=rope_tiles),
        grid=(m // tm, n // tn),
        in_specs=in_specs,
        out_specs=pl.BlockSpec((tm, tn), lambda i, j: (i, j)),
        out_shape=jax.ShapeDtypeStruct((m, n), out_dtype),
        scratch_shapes=[pltpu.VMEM((tm, d), BF16)],
        compiler_params=_params(("parallel", "arbitrary")),
        name="proj",
    )(*args)


def _proj_conv_kernel(xp_ref, x_ref, xn_ref, sh_ref, sc_ref, g_ref, w_ref, cw_ref, cb_ref, o_ref, h_ref,
                      *, tm, n_plain, tiles_per_seq):
    i = pl.program_id(0)
    j = pl.program_id(1)
    n_all = tm + 2 * CONV_HALO

    @pl.when(j == 0)
    def _():
        g, sh, sc = g_ref[...], sh_ref[...], sc_ref[...]
        t = i % tiles_per_seq
        keep_prev = (t != 0).astype(F32)
        keep_next = (t != tiles_per_seq - 1).astype(F32)
        h_ref[0:CONV_HALO, :] = (_modulate(xp_ref[...], g, sh, sc) * keep_prev).astype(BF16)
        h_ref[CONV_HALO:CONV_HALO + tm, :] = _modulate(x_ref[...], g, sh, sc).astype(BF16)
        h_ref[CONV_HALO + tm:n_all, :] = (_modulate(xn_ref[...], g, sh, sc) * keep_next).astype(BF16)

    @pl.when(j < n_plain)
    def _():
        o_ref[...] = jnp.dot(h_ref[CONV_HALO:CONV_HALO + tm, :], w_ref[...],
                             preferred_element_type=F32).astype(o_ref.dtype)

    @pl.when(j >= n_plain)
    def _():
        w = w_ref[...]
        step = n_all // CONV_ROW_BLOCKS if n_all % (CONV_ROW_BLOCKS * BF16_SUBLANES) == 0 else n_all
        u = jnp.concatenate([jnp.dot(h_ref[r0:r0 + step, :], w, preferred_element_type=F32)
                             for r0 in range(0, n_all, step)], axis=0)
        left = (SSM_CONV - 1) // 2
        acc = cb_ref[...] + cw_ref[left:left + 1, :] * u
        for tap in range(SSM_CONV):
            if tap == left:
                continue
            acc = acc + cw_ref[tap:tap + 1, :] * pltpu.roll(u, (left - tap) % n_all, 0)
        y = acc[CONV_HALO:CONV_HALO + tm, :]
        o_ref[...] = _silu(y).astype(o_ref.dtype)


def _proj_conv(x, mod_l, kinds, slot0, rows_per_slot, rows_per_seq, norm_g, w, n_plain_cols, conv_w, conv_b):
    m, d = x.shape
    n = n_plain_cols + conv_w.shape[1]
    tm = _pick(rows_per_seq, 1024)
    tn = _pick(math.gcd(n_plain_cols, n - n_plain_cols), 512)
    n_plain = n_plain_cols // tn
    tps = rows_per_seq // tm
    k_sh, k_sc = kinds
    hb = tm // CONV_HALO
    last_hb = m // CONV_HALO - 1
    in_specs = [
        pl.BlockSpec((CONV_HALO, d), lambda i, j: (jnp.maximum(i * hb - 1, 0), 0)),
        pl.BlockSpec((tm, d), lambda i, j: (i, 0)),
        pl.BlockSpec((CONV_HALO, d), lambda i, j: (jnp.minimum((i + 1) * hb, last_hb), 0)),
        _mod_spec(k_sh, slot0, rows_per_slot, tm, d),
        _mod_spec(k_sc, slot0, rows_per_slot, tm, d),
        pl.BlockSpec((1, d), lambda i, j: (0, 0)),
        pl.BlockSpec((d, tn), lambda i, j: (0, j)),
        pl.BlockSpec((SSM_CONV, tn), lambda i, j: (0, jnp.maximum(j - n_plain, 0))),
        pl.BlockSpec((1, tn), lambda i, j: (0, jnp.maximum(j - n_plain, 0))),
    ]
    return pl.pallas_call(
        functools.partial(_proj_conv_kernel, tm=tm, n_plain=n_plain, tiles_per_seq=tps),
        grid=(m // tm, n // tn),
        in_specs=in_specs,
        out_specs=pl.BlockSpec((tm, tn), lambda i, j: (i, j)),
        out_shape=jax.ShapeDtypeStruct((m, n), BF16),
        scratch_shapes=[pltpu.VMEM((tm + 2 * CONV_HALO, d), BF16)],
        compiler_params=_params(("parallel", "arbitrary")),
        name="proj_conv",
    )(x, x, x, mod_l, mod_l, norm_g.reshape(1, d), w, conv_w, conv_b.reshape(1, -1))


def _outproj_kernel(*refs, n_a):
    x_ref, gt_ref = refs[0], refs[1]
    a_refs = refs[2:2 + n_a]
    w_refs = refs[2 + n_a:2 + 2 * n_a]
    o_ref = refs[2 + 2 * n_a]
    acc = jnp.dot(a_refs[0][...], w_refs[0][...], preferred_element_type=F32)
    for a_ref, w_ref in zip(a_refs[1:], w_refs[1:]):
        acc = acc + jnp.dot(a_ref[...], w_ref[...], preferred_element_type=F32)
    o_ref[...] = x_ref[...] + gt_ref[...] * acc


def _outproj(x, mod_l, k_gt, slot0, rows_per_slot, a_list, w):
    m, d = x.shape
    tm = _pick(rows_per_slot, 1024)
    tn = _pick(d, 1024)
    in_specs = [pl.BlockSpec((tm, tn), lambda i, j: (i, j)), _mod_spec(k_gt, slot0, rows_per_slot, tm, tn, col=True)]
    args = [x, mod_l]
    k0 = 0
    w_specs = []
    for a in a_list:
        ka = a.shape[1]
        assert k0 % ka == 0
        in_specs.append(pl.BlockSpec((tm, ka), lambda i, j: (i, 0)))
        w_specs.append(pl.BlockSpec((ka, tn), lambda i, j, kb=k0 // ka: (kb, j)))
        k0 += ka
    assert k0 == w.shape[0]
    return pl.pallas_call(
        functools.partial(_outproj_kernel, n_a=len(a_list)),
        grid=(m // tm, d // tn),
        in_specs=in_specs + w_specs,
        out_specs=pl.BlockSpec((tm, tn), lambda i, j: (i, j)),
        out_shape=jax.ShapeDtypeStruct((m, d), F32),
        compiler_params=_params(("parallel", "parallel")),
        name="outproj",
    )(*args, *a_list, *([w] * len(a_list)))


def _na_plan(rows):
    kr = min(NA_WIN_ROWS, rows)
    qr = NA_QROWS if rows % NA_QROWS == 0 else 1
    slab = min(rows, qr + kr)
    starts, ids, layouts = [], [], []
    for b in range(rows // qr):
        ks = int(np.clip(qr * b - kr // 2, 0, rows - slab))
        lay = []
        for r in range(qr * b, qr * b + qr):
            rs = int(np.clip(r - kr // 2, 0, rows - kr))
            assert ks <= rs and rs + kr <= ks + slab
            lay.append((rs - ks, r - ks))
        lay = tuple(lay)
        if lay not in layouts:
            layouts.append(lay)
        starts.append(ks)
        ids.append(layouts.index(lay))
    return kr, qr, slab, np.array([starts, ids], np.int32), tuple(layouts)


def _na_bias_kernel(rpb_ref, o_ref, *, kr, slab, layouts):
    h = pl.program_id(0)
    n_ro, n_co = 2 * NA_WIN_ROWS - 1, 2 * NA_WIN_COLS - 1
    qc = lax.broadcasted_iota(jnp.int32, (GRID_W, GRID_W), 0)
    kc = lax.broadcasted_iota(jnp.int32, (GRID_W, GRID_W), 1)
    dcol = kc - qc + (NA_WIN_COLS - 1)
    cs = jnp.clip(qc - NA_WIN_COLS // 2, 0, GRID_W - NA_WIN_COLS)
    valid = jnp.logical_and(kc >= cs, kc < cs + NA_WIN_COLS)
    tabs = []
    for ro in range(n_ro):
        acc = jnp.zeros((GRID_W, GRID_W), F32)
        for dd in range(n_co):
            acc = jnp.where(dcol == dd, rpb_ref[(h * n_ro + ro) * n_co + dd], acc)
        tabs.append(jnp.where(valid, acc * LOG2E, NEG_BIG))
    masked = jnp.full((GRID_W, GRID_W), NEG_BIG, F32)
    for ti, lay in enumerate(layouts):
        for qi, (win0, qrow) in enumerate(lay):
            for ki in range(slab):
                ro = ki - qrow + (NA_WIN_ROWS - 1)
                tile = tabs[ro] if win0 <= ki < win0 + kr else masked
                o_ref[ti, qi * GRID_W:(qi + 1) * GRID_W, ki * GRID_W:(ki + 1) * GRID_W] = tile


def _na_bias(rpb, plan):
    kr, qr, slab, _, layouts = plan
    nh = rpb.shape[0]
    shape = (len(layouts), qr * GRID_W, slab * GRID_W)
    return pl.pallas_call(
        functools.partial(_na_bias_kernel, kr=kr, slab=slab, layouts=layouts),
        grid=(nh,),
        in_specs=[pl.BlockSpec(memory_space=pltpu.SMEM)],
        out_specs=pl.BlockSpec((None,) + shape, lambda h: (h, 0, 0, 0)),
        out_shape=jax.ShapeDtypeStruct((nh,) + shape, F32),
        compiler_params=_params(("parallel",)),
        name="na_bias",
    )(rpb.reshape(-1))


def _dot_t(a, b):
    return lax.dot_general(a, b, (((1,), (1,)), ((), ())), preferred_element_type=F32)


def _na_kernel(plan_ref, q_ref, k_ref, v_ref, kc_ref, vc_ref, bias_ref, o_ref, *, n_blocks, nq, nk):
    c = HEAD_DIM ** -0.5 * LOG2E
    kc = kc_ref[...]
    vc = vc_ref[...]

    def block(blk, carry):
        q0 = pl.multiple_of(blk * nq, nq)
        k0 = pl.multiple_of(plan_ref[0, blk] * GRID_W, GRID_W)
        q = q_ref[pl.ds(q0, nq), :]
        s = _dot_t(q, k_ref[pl.ds(k0, nk), :]) * c + bias_ref[plan_ref[1, blk]]
        sc = _dot_t(q, kc) * c
        m = jnp.maximum(jnp.max(s, axis=-1, keepdims=True), jnp.max(sc, axis=-1, keepdims=True))
        p = jnp.exp2(s - m)
        pc = jnp.exp2(sc - m)
        inv = 1.0 / (jnp.sum(p, axis=-1, keepdims=True) + jnp.sum(pc, axis=-1, keepdims=True))
        o = (jnp.dot(p.astype(BF16), v_ref[pl.ds(k0, nk), :], preferred_element_type=F32)
             + jnp.dot(pc.astype(BF16), vc, preferred_element_type=F32)) * inv
        o_ref[pl.ds(q0, nq), :] = o.astype(o_ref.dtype)
        return carry

    lax.fori_loop(0, n_blocks, block, 0, unroll=NA_BLOCK_UNROLL)


def _na_attention(qkv_x, qkv_c, bias, plan, batch, seq, ctx_len, n_heads):
    _, qr, slab, plan_arr, layouts = plan
    tab = (len(layouts), qr * GRID_W, slab * GRID_W)
    return pl.pallas_call(
        functools.partial(_na_kernel, n_blocks=plan_arr.shape[1], nq=qr * GRID_W, nk=slab * GRID_W),
        grid=(n_heads, batch),
        in_specs=[
            pl.BlockSpec(memory_space=pltpu.SMEM),
            pl.BlockSpec((seq, HEAD_DIM), lambda h, b: (b, h)),
            pl.BlockSpec((seq, HEAD_DIM), lambda h, b: (b, n_heads + h)),
            pl.BlockSpec((seq, HEAD_DIM), lambda h, b: (b, 2 * n_heads + h)),
            pl.BlockSpec((ctx_len, HEAD_DIM), lambda h, b: (b, n_heads + h)),
            pl.BlockSpec((ctx_len, HEAD_DIM), lambda h, b: (b, 2 * n_heads + h)),
            pl.BlockSpec((None,) + tab, lambda h, b: (h, 0, 0, 0)),
        ],
        out_specs=pl.BlockSpec((seq, HEAD_DIM), lambda h, b: (b, h)),
        out_shape=jax.ShapeDtypeStruct((batch * seq, n_heads * HEAD_DIM), BF16),
        compiler_params=_params(("parallel", "parallel")),
        name="na_attn",
    )(jnp.asarray(plan_arr), qkv_x, qkv_x, qkv_x, qkv_c, qkv_c, bias)


def _ctx_attn_kernel(q_ref, k_ref, v_ref, o_ref):
    s = _dot_t(q_ref[...], k_ref[...]) * HEAD_DIM ** -0.5
    p = jnp.exp(s - jnp.max(s, axis=-1, keepdims=True))
    p = p * (1.0 / jnp.sum(p, axis=-1, keepdims=True))
    o_ref[...] = jnp.dot(p.astype(BF16), v_ref[...], preferred_element_type=F32).astype(o_ref.dtype)


def _ctx_attention(qkv_c, batch, ctx_len, n_heads):
    return pl.pallas_call(
        _ctx_attn_kernel,
        grid=(batch, n_heads),
        in_specs=[
            pl.BlockSpec((ctx_len, HEAD_DIM), lambda b, h: (b, h)),
            pl.BlockSpec((ctx_len, HEAD_DIM), lambda b, h: (b, n_heads + h)),
            pl.BlockSpec((ctx_len, HEAD_DIM), lambda b, h: (b, 2 * n_heads + h)),
        ],
        out_specs=pl.BlockSpec((ctx_len, HEAD_DIM), lambda b, h: (b, h)),
        out_shape=jax.ShapeDtypeStruct((batch * ctx_len, n_heads * HEAD_DIM), BF16),
        compiler_params=_params(("parallel", "parallel")),
        name="ctx_attn",
    )(qkv_c, qkv_c, qkv_c)


def _diff_kernel(*refs, n_src, lam_init, src_rows):
    lam_ref, g_ref, q_ref = refs[0], refs[1], refs[2]
    k_refs = refs[3:3 + n_src]
    v_refs = refs[3 + n_src:3 + 2 * n_src]
    o_ref = refs[3 + 2 * n_src]
    acc_ref = refs[4 + 2 * n_src]
    tq = q_ref.shape[0]
    c = HEAD_DIM ** -0.5 * LOG2E
    lf = lam_ref[...]
    lam = (jnp.exp(jnp.sum(lf[0:1] * lf[1:2], axis=-1, keepdims=True))
           - jnp.exp(jnp.sum(lf[2:3] * lf[3:4], axis=-1, keepdims=True)) + lam_init)
    qs = [q_ref[:, mi * HEAD_DIM:(mi + 1) * HEAD_DIM] for mi in range(2)]
    acc_ref[...] = jnp.zeros_like(acc_ref)

    def chunk(carry, k_ref, v_ref, r0, n):
        rows = pl.ds(r0, n)
        v = v_ref[rows, :]
        out = []
        for mi in range(2):
            m, l = carry[2 * mi], carry[2 * mi + 1]
            s = _dot_t(qs[mi], k_ref[rows, mi * HEAD_DIM:(mi + 1) * HEAD_DIM])
            m_new = jnp.maximum(m, jnp.max(s, axis=-1, keepdims=True))
            a = jnp.exp2((m - m_new) * c)
            p = jnp.exp2((s - m_new) * c)
            acc_ref[mi] = acc_ref[mi] * a + jnp.dot(p.astype(BF16), v, preferred_element_type=F32)
            out += [m_new, l * a + jnp.sum(p, axis=-1, keepdims=True)]
        return tuple(out)

    carry = (jnp.full((tq, 1), NEG_BIG, F32), jnp.zeros((tq, 1), F32)) * 2
    for k_ref, v_ref, rows in zip(k_refs, v_refs, src_rows):
        ck = _pick(rows, DIFF_KEY_CHUNK)
        if rows == ck:
            carry = chunk(carry, k_ref, v_ref, 0, ck)
        else:
            carry = lax.fori_loop(
                0, rows // ck,
                lambda i, cr, k_ref=k_ref, v_ref=v_ref, ck=ck: chunk(cr, k_ref, v_ref, pl.multiple_of(i * ck, ck), ck),
                carry, unroll=True)
    _, l1, _, l2 = carry
    o = acc_ref[0] * (1.0 / l1) - acc_ref[1] * (lam / l2)
    ms = jnp.mean(o * o, axis=-1, keepdims=True)
    o_ref[...] = ((o * lax.rsqrt(ms + EPS)) * g_ref[...] * (1.0 - lam_init)).astype(o_ref.dtype)


def _diff_attention(q_arr, q_rows, kv_list, lam_vecs, subln_g, lam_init, batch, n_heads, q_col0, tq_want):
    hw = 2 * HEAD_DIM
    tq = _pick(q_rows, tq_want)
    nq = q_rows // tq
    qb = q_col0 // hw
    in_specs = [
        pl.BlockSpec((4, HEAD_DIM), lambda b, h, i: (0, 0)),
        pl.BlockSpec((1, hw), lambda b, h, i: (0, 0)),
        pl.BlockSpec((tq, hw), lambda b, h, i: (b * nq + i, qb + h)),
    ]
    args = [lam_vecs, subln_g.reshape(1, hw), q_arr]
    for arr, rows in kv_list:
        in_specs.append(pl.BlockSpec((rows, hw), lambda b, h, i: (b, qb + n_heads + h)))
        args.append(arr)
    for arr, rows in kv_list:
        in_specs.append(pl.BlockSpec((rows, hw), lambda b, h, i: (b, qb + 2 * n_heads + h)))
        args.append(arr)
    return pl.pallas_call(
        functools.partial(_diff_kernel, n_src=len(kv_list), lam_init=lam_init,
                          src_rows=tuple(rows for _, rows in kv_list)),
        grid=(batch, n_heads, nq),
        in_specs=in_specs,
        out_specs=pl.BlockSpec((tq, hw), lambda b, h, i: (b * nq + i, h)),
        out_shape=jax.ShapeDtypeStruct((batch * q_rows, n_heads * hw), BF16),
        scratch_shapes=[pltpu.VMEM((2, tq, hw), F32)],
        compiler_params=_params(("parallel", "parallel", "parallel")),
        name="diff_attn",
    )(*args)


def _cumsum_rows(a):
    n = a.shape[0]
    row = lax.broadcasted_iota(jnp.int32, a.shape, 0)
    d = 1
    while d < n:
        a = a + jnp.where(row >= d, pltpu.roll(a, d, 0), 0.0)
        d *= 2
    return a


def _ssd_kernel(*refs, n_ctx_chunks, n_x_chunks, hpg, n_heads, ctx_out):
    (xs_x, z_x, b_x, c_x, dt_x, xs_c, z_c, b_c, c_c, dt_c, bias_ref, alog_ref, dsum_ref, ng_ref) = refs[:14]
    n_out = 2 if ctx_out else 1
    yx_ref = refs[14]
    yc_ref = refs[15] if ctx_out else None
    yacc_x, yacc_c, sc_x, sc_c, rft_x, rft_c, hf_ref, hb_ref = refs[14 + n_out:]
    t = SSM_CHUNK
    pw = 2 * SSM_HEADDIM
    g = pl.program_id(1)
    nl = 2 * n_heads

    lane = lax.broadcasted_iota(jnp.int32, (t, nl), 1)
    is_fwd = lane < n_heads
    bias = bias_ref[...]
    neg_a2 = -jnp.exp(alog_ref[...]) * LOG2E
    li = lax.broadcasted_iota(jnp.int32, (t, t), 0)
    si = lax.broadcasted_iota(jnp.int32, (t, t), 1)
    add_lo = jnp.where(li >= si, 0.0, NEG_BIG)
    add_up = jnp.where(li <= si, 0.0, NEG_BIG)
    pair_lane = lax.broadcasted_iota(jnp.int32, (t, pw), 1) < SSM_HEADDIM
    lane_shift = (nl - g * hpg) % nl

    def pair_weights(xp):
        zero = jnp.zeros_like(xp)
        return jnp.concatenate([jnp.where(pair_lane, xp, zero), jnp.where(pair_lane, zero, xp)], axis=0)

    def col(scl, idx):
        return jnp.broadcast_to(scl[:, idx:idx + 1], (t, t))

    def fwd_chunk(xs_ref, b_ref, c_ref, dt_ref, yacc, sc_ref, rft_ref, r0, h0):
        rows = pl.ds(r0, t)
        dt = jax.nn.softplus(dt_ref[rows, :] + bias)
        a2 = dt * neg_a2
        cs = _cumsum_rows(a2)
        sc2 = jnp.where(is_fwd, cs, cs[t - 1:t, :] - cs + a2)
        sc_ref[rows, :] = sc2
        rft_ref[pl.ds(h0, nl), :] = (sc2 - jnp.log(dt) * LOG2E).T
        scl = pltpu.roll(sc2, lane_shift, 1)
        xb = xs_ref[rows, :]
        bb = b_ref[rows, :]
        cb_ = c_ref[rows, :]
        cbm = _dot_t(cb_, bb)
        bt = bb.astype(F32).T
        yoff = jnp.dot(cb_, hf_ref[...].astype(BF16), preferred_element_type=F32)
        for jp in range(hpg // 2):
            ms, bws, es = [], [], []
            for j in (2 * jp, 2 * jp + 1):
                cf = col(scl, j)
                cbk = col(scl, n_heads + j)
                rf = rft_ref[pl.ds(h0 + g * hpg + j, 1), :]
                rb = rft_ref[pl.ds(h0 + n_heads + g * hpg + j, 1), :]
                ms.append((cbm * (jnp.exp2(cf - rf + add_lo) + jnp.exp2(cbk - rb + add_up))).astype(BF16))
                bws.append((bt * jnp.exp2(cf[t - 1:t, :] - rf)).astype(BF16))
                es.append(jnp.exp2(cf))
            cols = slice(jp * pw, (jp + 1) * pw)
            xp = xb[:, cols]
            lhs = jnp.concatenate([jnp.concatenate(ms, axis=1), jnp.concatenate(bws, axis=1)], axis=0)
            res = jnp.dot(lhs, pair_weights(xp), preferred_element_type=F32)
            e_pair = jnp.where(pair_lane, es[0], es[1])
            yacc[rows, cols] = res[:t] + xp.astype(F32) * dsum_ref[:, cols] + yoff[:, cols] * e_pair
            hf_ref[:, cols] = hf_ref[:, cols] * e_pair[t - 1:t, :] + res[t:]

    def bwd_chunk(xs_ref, z_ref, b_ref, c_ref, yacc, sc_ref, rft_ref, y_ref, r0, h0):
        rows = pl.ds(r0, t)
        scl = pltpu.roll(sc_ref[rows, :], lane_shift, 1)
        xb = xs_ref[rows, :]
        bt = b_ref[rows, :].astype(F32).T
        yoff = jnp.dot(c_ref[rows, :], hb_ref[...].astype(BF16), preferred_element_type=F32)
        ys = []
        for jp in range(hpg // 2):
            bws, es = [], []
            for j in (2 * jp, 2 * jp + 1):
                cbk = col(scl, n_heads + j)
                rb = rft_ref[pl.ds(h0 + n_heads + g * hpg + j, 1), :]
                bws.append((bt * jnp.exp2(cbk[0:1, :] - rb)).astype(BF16))
                es.append(jnp.exp2(cbk))
            cols = slice(jp * pw, (jp + 1) * pw)
            grow = jnp.dot(jnp.concatenate(bws, axis=1), pair_weights(xb[:, cols]), preferred_element_type=F32)
            e_pair = jnp.where(pair_lane, es[0], es[1])
            ys.append(yacc[rows, cols] + yoff[:, cols] * e_pair)
            hb_ref[:, cols] = hb_ref[:, cols] * e_pair[0:1, :] + grow
        if y_ref is not None:
            yg = jnp.concatenate(ys, axis=1) * _silu(z_ref[rows, :].astype(F32))
            ms = jnp.mean(yg * yg, axis=-1, keepdims=True)
            y_ref[rows, :] = ((yg * lax.rsqrt(ms + EPS)) * ng_ref[...]).astype(y_ref.dtype)

    hf_ref[...] = jnp.zeros_like(hf_ref)
    hb_ref[...] = jnp.zeros_like(hb_ref)
    for k in range(n_ctx_chunks):
        fwd_chunk(xs_c, b_c, c_c, dt_c, yacc_c, sc_c, rft_c, k * t, k * nl)

    def fx(k, carry):
        fwd_chunk(xs_x, b_x, c_x, dt_x, yacc_x, sc_x, rft_x, pl.multiple_of(k * t, t), pl.multiple_of(k * nl, nl))
        return carry

    lax.fori_loop(0, n_x_chunks, fx, 0, unroll=SSD_CHUNK_UNROLL)
    for k in reversed(range(n_ctx_chunks)):
        bwd_chunk(xs_c, z_c, b_c, c_c, yacc_c, sc_c, rft_c, yc_ref, k * t, k * nl)

    def bx(k, carry):
        kk = n_x_chunks - 1 - k
        bwd_chunk(xs_x, z_x, b_x, c_x, yacc_x, sc_x, rft_x, yx_ref, pl.multiple_of(kk * t, t),
                  pl.multiple_of(kk * nl, nl))
        return carry

    lax.fori_loop(0, n_x_chunks, bx, 0, unroll=SSD_CHUNK_UNROLL)


def _ssd(zx, zc, dtx, dtc, a_log, dt_bias, d_skip, norm_g, batch, seq, ctx_len, ctx_out):
    inner = norm_g.shape[0]
    n_heads = inner // SSM_HEADDIM
    hpg = n_heads // SSM_GROUPS
    gw = hpg * SSM_HEADDIM
    nl = 2 * n_heads
    assert hpg % 2 == 0 and seq % SSM_CHUNK == 0 and ctx_len % SSM_CHUNK == 0
    assert SSM_CHUNK == 2 * SSM_HEADDIM == SSM_STATE == V7X_LANES and nl == V7X_LANES
    zb = inner // gw
    b0 = (2 * inner) // SSM_STATE
    c0 = b0 + SSM_GROUPS

    def specs(rows):
        return [
            pl.BlockSpec((rows, gw), lambda b, g: (b, zb + g)),
            pl.BlockSpec((rows, gw), lambda b, g: (b, g)),
            pl.BlockSpec((rows, SSM_STATE), lambda b, g: (b, b0 + g)),
            pl.BlockSpec((rows, SSM_STATE), lambda b, g: (b, c0 + g)),
            pl.BlockSpec((rows, nl), lambda b, g: (b, 0)),
        ]

    dsum = jnp.repeat(d_skip[0] + d_skip[1], SSM_HEADDIM).reshape(1, inner).astype(F32)
    in_specs = specs(seq) + specs(ctx_len) + [
        pl.BlockSpec((1, nl), lambda b, g: (0, 0)),
        pl.BlockSpec((1, nl), lambda b, g: (0, 0)),
        pl.BlockSpec((1, gw), lambda b, g: (0, g)),
        pl.BlockSpec((1, gw), lambda b, g: (0, g)),
    ]
    out_specs = [pl.BlockSpec((seq, gw), lambda b, g: (b, g))]
    out_shape = [jax.ShapeDtypeStruct((batch * seq, inner), BF16)]
    if ctx_out:
        out_specs.append(pl.BlockSpec((ctx_len, gw), lambda b, g: (b, g)))
        out_shape.append(jax.ShapeDtypeStruct((batch * ctx_len, inner), BF16))
    ncx, ncc = seq // SSM_CHUNK, ctx_len // SSM_CHUNK
    scratch = [
        pltpu.VMEM((seq, gw), F32), pltpu.VMEM((ctx_len, gw), F32),
        pltpu.VMEM((seq, nl), F32), pltpu.VMEM((ctx_len, nl), F32),
        pltpu.VMEM((ncx * nl, SSM_CHUNK), F32), pltpu.VMEM((ncc * nl, SSM_CHUNK), F32),
        pltpu.VMEM((SSM_STATE, gw), F32), pltpu.VMEM((SSM_STATE, gw), F32),
    ]
    outs = pl.pallas_call(
        functools.partial(_ssd_kernel, n_ctx_chunks=ctx_len // SSM_CHUNK, n_x_chunks=seq // SSM_CHUNK,
                          hpg=hpg, n_heads=n_heads, ctx_out=ctx_out),
        grid=(batch, SSM_GROUPS),
        in_specs=in_specs,
        out_specs=out_specs,
        out_shape=out_shape,
        scratch_shapes=scratch,
        compiler_params=_params(("parallel", "parallel")),
        name="ssd",
    )(zx, zx, zx, zx, dtx, zc, zc, zc, zc, dtc,
      dt_bias.reshape(1, nl), a_log.reshape(1, nl), dsum, norm_g.reshape(1, inner))
    return (outs[0], outs[1]) if ctx_out else (outs[0], None)


def _rope_tables(seq):
    quarter = HEAD_DIM // 4
    inv = 1.0 / (ROPE_THETA ** (jnp.arange(quarter, dtype=F32) / quarter))
    t = jnp.arange(seq)
    row = (t // GRID_W).astype(F32)[:, None] * inv
    col = (t % GRID_W).astype(F32)[:, None] * inv
    ang = jnp.concatenate([row, row, col, col], axis=-1)
    cos, sin = jnp.cos(ang), jnp.sin(ang)
    first = (np.arange(HEAD_DIM) % (2 * quarter)) < quarter
    return cos, jnp.where(first, -sin, 0.0), jnp.where(first, 0.0, sin)


def kernel(x, c, ctx, c_ctx, mod_w, mod_b, norm_g, ffn_w1, ffn_w3, ffn_w2, attn_w_in, attn_w_out, na_rpb,
           diff_lambda, diff_subln_g, ssm_w_in, ssm_conv_w, ssm_conv_b, ssm_a_log, ssm_dt_bias, ssm_d,
           ssm_norm_g, ssm_w_out, final_norm_g):
    batch, seq, d = x.shape
    ctx_len = ctx.shape[1]
    depth = mod_w.shape[0]
    assert batch + 1 <= MOD_SLOTS
    na_w = d // 2
    na_heads = na_w // HEAD_DIM
    diff_heads = na_w // (2 * HEAD_DIM)
    inner = ssm_norm_g.shape[1]
    n_ssm_heads = inner // SSM_HEADDIM

    cvec = jnp.zeros((MOD_SLOTS, d), F32).at[:batch].set(c).at[batch].set(c_ctx)
    mod = _modulation(cvec, mod_w, mod_b)
    mod = mod.reshape(depth, MOD_SLOTS, N_MOD, d).transpose(0, 2, 1, 3)[:, :, :, None, :]

    xs = x.reshape(batch * seq, d)
    hs = ctx.reshape(batch * ctx_len, d)
    rope = _rope_tables(seq)
    na_plan = _na_plan(seq // GRID_W)
    xslot = (0, seq)
    cslot = (batch, batch * ctx_len)
    ffn_w = tuple(w.astype(BF16) for w in (ffn_w1, ffn_w3, ffn_w2))

    for layer in range(depth):
        ctx_out = layer < depth - 1
        last = layer == depth - 1
        ml = mod[layer]
        ng = norm_g[layer]
        xs = _ffn(xs, ml, (0, 1, 2), *xslot, ng[0], *ffn_w, (layer, 0))
        hs = _ffn(hs, ml, (0, 1, 2), *cslot, ng[0], *ffn_w, (layer, 0))

        i = layer // 2
        if layer % 2 == 0:
            lam_init = 0.8 - 0.6 * math.exp(-0.3 * layer)
            w_in = attn_w_in[i].astype(BF16)
            w_out = attn_w_out[i].astype(BF16)
            dq0 = 3 * na_w
            qkv_x = _proj(xs, ml, (3, 4), *xslot, ng[1], w_in, BF16, 1024, rope=(rope, (dq0, dq0 + 2 * na_w)))
            qkv_c = _proj(hs, ml, (3, 4), *cslot, ng[1], w_in, BF16, 1024)
            bias = _na_bias(na_rpb[i], na_plan)
            na_x = _na_attention(qkv_x, qkv_c, bias, na_plan, batch, seq, ctx_len, na_heads)
            df_x = _diff_attention(qkv_x, seq, [(qkv_x, seq), (qkv_c, ctx_len)], diff_lambda[i], diff_subln_g[i],
                                   lam_init, batch, diff_heads, dq0, 512)
            xs = _outproj(xs, ml, 5, *xslot, [na_x, df_x], w_out)
            if ctx_out:
                na_c = _ctx_attention(qkv_c, batch, ctx_len, na_heads)
                df_c = _diff_attention(qkv_c, ctx_len, [(qkv_c, ctx_len)], diff_lambda[i], diff_subln_g[i],
                                       lam_init, batch, diff_heads, dq0, 256)
                hs = _outproj(hs, ml, 5, *cslot, [na_c, df_c], w_out)
        else:
            n_zx = 2 * inner + 2 * SSM_GROUPS * SSM_STATE
            w_in = ssm_w_in[i].astype(BF16)
            dt_cols = (n_zx, w_in.shape[1])
            cw, cb = ssm_conv_w[i], ssm_conv_b[i]
            zx = _proj_conv(xs, ml, (3, 4), *xslot, seq, ng[1], w_in, inner, cw, cb)
            zc = _proj_conv(hs, ml, (3, 4), *cslot, ctx_len, ng[1], w_in, inner, cw, cb)
            dtx = _proj(xs, ml, (3, 4), *xslot, ng[1], w_in, F32, 2 * n_ssm_heads, cols=dt_cols)
            dtc = _proj(hs, ml, (3, 4), *cslot, ng[1], w_in, F32, 2 * n_ssm_heads, cols=dt_cols)
            yx, yc = _ssd(zx, zc, dtx, dtc, ssm_a_log[i], ssm_dt_bias[i], ssm_d[i], ssm_norm_g[i],
                          batch, seq, ctx_len, ctx_out)
            w_out = ssm_w_out[i].astype(BF16)
            xs = _outproj(xs, ml, 5, *xslot, [yx], w_out)
            if ctx_out:
                hs = _outproj(hs, ml, 5, *cslot, [yc], w_out)

        xs = _ffn(xs, ml, (6, 7, 8), *xslot, ng[2], *ffn_w, (layer, 1), final_g=final_norm_g if last else None)
        if ctx_out:
            hs = _ffn(hs, ml, (6, 7, 8), *cslot, ng[2], *ffn_w, (layer, 1))
    return xs.reshape(batch, seq, d)
```
